```python
import math
import jax, jax.numpy as jnp
from jax import lax
import numpy as np

D_MODEL = 2048
BATCH = 2
SEQ = 4096
DEPTH = 1

CHUNK = 64
N_META = 16
Q_BLOCK = 128
ATTN_WIDTH = D_MODEL // 2
CONV_WIDTH = D_MODEL // 2
MIX_WIDTH = ATTN_WIDTH + CONV_WIDTH
DA_HEAD_DIM = 128
DA_N_HEADS = ATTN_WIDTH // DA_HEAD_DIM
DA_QK_DIM = DA_HEAD_DIM // 2
CONV_KERNEL = 31
D_FF = 5632
PROJ_WIDTH = 3 * ATTN_WIDTH + 2 * CONV_WIDTH
EPS = 1e-6
NEG_INF = -1e30

kernel_name = "hybrid_diffattn_conformer_conv_macaron"


def lambda_init_fn(layer_idx):
    return 0.8 - 0.6 * math.exp(-0.3 * layer_idx)


def rms_norm(x, g):
    xf = x.astype(jnp.float32)
    y = xf * lax.rsqrt(jnp.mean(xf * xf, axis=-1, keepdims=True) + EPS)
    return (y * g.astype(jnp.float32)).astype(x.dtype)


def swiglu(h, w_gate, w_up, w_down):
    return (jax.nn.silu(h @ w_gate) * (h @ w_up)) @ w_down


def chunk_ids(length):
    pos = jnp.arange(length)
    return jnp.where(pos < N_META, 0, 1 + (pos - N_META) // CHUNK)


def diff_attention(q, k, v, lam, lam_init, subln_g):
    B, L, H = q.shape[0], q.shape[1], q.shape[2]
    n_blk = -(-L // Q_BLOCK)
    Lp = n_blk * Q_BLOCK
    pad = Lp - L
    qp = jnp.pad(q, ((0, 0), (0, pad), (0, 0), (0, 0), (0, 0)))
    kp = jnp.pad(k, ((0, 0), (0, pad), (0, 0), (0, 0), (0, 0)))
    vp = jnp.pad(v, ((0, 0), (0, pad), (0, 0), (0, 0)))
    cid = chunk_ids(Lp)
    qb = qp.reshape(B, n_blk, Q_BLOCK, H, 2, DA_QK_DIM).transpose(1, 0, 2, 3, 4, 5)
    qcid = cid.reshape(n_blk, Q_BLOCK)
    scale = DA_QK_DIM ** -0.5

    def block(args):
        q_blk, cq = args
        s = jnp.einsum('bqhcd,bkhcd->bhcqk', q_blk, kp).astype(jnp.float32) * scale
        visible = cid[None, :] <= cq[:, None]
        s = jnp.where(visible[None, None, None], s, NEG_INF)
        p = jax.nn.softmax(s, axis=-1)
        attn = p[:, :, 0] - lam * p[:, :, 1]
        return jnp.einsum('bhqk,bkhd->bqhd', attn.astype(vp.dtype), vp)

    o = lax.map(block, (qb, qcid))
    o = o.transpose(1, 0, 2, 3, 4).reshape(B, Lp, H, DA_HEAD_DIM)[:, :L]
    o = rms_norm(o, subln_g) * (1.0 - lam_init)
    return o.reshape(B, L, ATTN_WIDTH)


def conformer_conv(a, gate, conv_w, conv_b, norm_g):
    u = a * jax.nn.sigmoid(gate)
    y = lax.conv_general_dilated(
        u, conv_w.astype(u.dtype), window_strides=(1,), padding=[(CONV_KERNEL - 1, 0)],
        dimension_numbers=('NWC', 'WIO', 'NWC'), feature_group_count=CONV_WIDTH)
    y = y + conv_b
    y = rms_norm(y, norm_g)
    return jax.nn.silu(y)


def setup_inputs(seed: int = 0) -> dict:
    key = jax.random.key(seed)
    ks = jax.random.split(key, 24)
    f32 = jnp.float32

    def nrm(k, shape, fan_in):
        return jax.random.normal(k, shape, f32) * (fan_in ** -0.5)

    def gain(k, shape):
        return 1.0 + 0.01 * jax.random.normal(k, shape, f32)

    return {
        "x": jax.random.normal(ks[0], (BATCH, SEQ, D_MODEL), f32),
        "meta_tokens": jax.random.normal(ks[1], (N_META, D_MODEL), f32),
        "ffn1_norm_g": gain(ks[2], (DEPTH, D_MODEL)),
        "ffn1_w_gate": nrm(ks[3], (DEPTH, D_MODEL, D_FF), D_MODEL),
        "ffn1_w_up": nrm(ks[4], (DEPTH, D_MODEL, D_FF), D_MODEL),
        "ffn1_w_down": nrm(ks[5], (DEPTH, D_FF, D_MODEL), D_FF),
        "mix_norm_g": gain(ks[6], (DEPTH, D_MODEL)),
        "w_in": nrm(ks[7], (DEPTH, D_MODEL, PROJ_WIDTH), D_MODEL),
        "q_norm_g": gain(ks[8], (DEPTH, DA_QK_DIM)),
        "k_norm_g": gain(ks[9], (DEPTH, DA_QK_DIM)),
        "lambda_q1": 0.1 * jax.random.normal(ks[10], (DEPTH, DA_QK_DIM), f32),
        "lambda_k1": 0.1 * jax.random.normal(ks[11], (DEPTH, DA_QK_DIM), f32),
        "lambda_q2": 0.1 * jax.random.normal(ks[12], (DEPTH, DA_QK_DIM), f32),
        "lambda_k2": 0.1 * jax.random.normal(ks[13], (DEPTH, DA_QK_DIM), f32),
        "attn_subln_g": gain(ks[14], (DEPTH, DA_HEAD_DIM)),
        "conv_w": nrm(ks[15], (DEPTH, CONV_KERNEL, 1, CONV_WIDTH), CONV_KERNEL),
        "conv_b": 0.01 * jax.random.normal(ks[16], (DEPTH, CONV_WIDTH), f32),
        "conv_norm_g": gain(ks[17], (DEPTH, CONV_WIDTH)),
        "w_out": nrm(ks[18], (DEPTH, MIX_WIDTH, D_MODEL), MIX_WIDTH),
        "ffn2_norm_g": gain(ks[19], (DEPTH, D_MODEL)),
        "ffn2_w_gate": nrm(ks[20], (DEPTH, D_MODEL, D_FF), D_MODEL),
        "ffn2_w_up": nrm(ks[21], (DEPTH, D_MODEL, D_FF), D_MODEL),
        "ffn2_w_down": nrm(ks[22], (DEPTH, D_FF, D_MODEL), D_FF),
        "final_norm_g": gain(ks[23], (DEPTH, D_MODEL)),
    }


def reference(x, meta_tokens, ffn1_norm_g, ffn1_w_gate, ffn1_w_up, ffn1_w_down,
              mix_norm_g, w_in, q_norm_g, k_norm_g, lambda_q1, lambda_k1, lambda_q2,
              lambda_k2, attn_subln_g, conv_w, conv_b, conv_norm_g, w_out,
              ffn2_norm_g, ffn2_w_gate, ffn2_w_up, ffn2_w_down, final_norm_g):
    B = x.shape[0]
    meta = jnp.broadcast_to(meta_tokens[None].astype(x.dtype), (B, N_META, D_MODEL))
    h_stream = jnp.concatenate([meta, x], axis=1)
    L = h_stream.shape[1]
    split_pts = [ATTN_WIDTH, 2 * ATTN_WIDTH, 3 * ATTN_WIDTH, 3 * ATTN_WIDTH + CONV_WIDTH]

    for l in range(DEPTH):
        lam_init = lambda_init_fn(l + 1)
        h = rms_norm(h_stream, ffn1_norm_g[l])
        h_stream = h_stream + 0.5 * swiglu(h, ffn1_w_gate[l], ffn1_w_up[l], ffn1_w_down[l])

        h = rms_norm(h_stream, mix_norm_g[l])
        proj = h @ w_in[l]
        q, k, v, ca, cg = jnp.split(proj, split_pts, axis=-1)
        q = rms_norm(q.reshape(B, L, DA_N_HEADS, 2, DA_QK_DIM), q_norm_g[l])
        k = rms_norm(k.reshape(B, L, DA_N_HEADS, 2, DA_QK_DIM), k_norm_g[l])
        v = v.reshape(B, L, DA_N_HEADS, DA_HEAD_DIM)
        lam = (jnp.exp(jnp.sum(lambda_q1[l].astype(jnp.float32) * lambda_k1[l].astype(jnp.float32)))
               - jnp.exp(jnp.sum(lambda_q2[l].astype(jnp.float32) * lambda_k2[l].astype(jnp.float32)))
               + lam_init)
        o_att = diff_attention(q, k, v, lam, lam_init, attn_subln_g[l])
        o_conv = conformer_conv(ca, cg, conv_w[l], conv_b[l], conv_norm_g[l])
        h_stream = h_stream + jnp.concatenate([o_att, o_conv], axis=-1) @ w_out[l]

        h = rms_norm(h_stream, ffn2_norm_g[l])
        h_stream = h_stream + 0.5 * swiglu(h, ffn2_w_gate[l], ffn2_w_up[l], ffn2_w_down[l])
        h_stream = rms_norm(h_stream, final_norm_g[l])

    return h_stream[:, N_META:]
```

```python
import functools
import math

import jax
import jax.numpy as jnp
from jax import lax
from jax.experimental import pallas as pl
from jax.experimental.pallas import tpu as pltpu

F32 = jnp.float32
BF16 = jnp.bfloat16

D_MODEL = 2048
SEQ = 4096
N_META = 16
CHUNK = 64
ATTN_WIDTH = 1024
CONV_WIDTH = 1024
HEAD_DIM = 128
N_HEADS = ATTN_WIDTH // HEAD_DIM
QK_DIM = HEAD_DIM // 2
CONV_KERNEL = 31
D_FF = 5632
EPS = 1e-6
NEG_INF = -1e30
LAM_INIT = 0.8 - 0.6 * math.exp(-0.3 * 1)

LANES = 128
HALO = 32
VMEM_LIMIT = 56 * 1024 * 1024

NT_DIMS = (((1,), (1,)), ((), ()))


def _params(*sem):
    return pltpu.CompilerParams(dimension_semantics=sem, vmem_limit_bytes=VMEM_LIMIT)


def _rms(x, g):
    ms = jnp.mean(x * x, axis=-1, keepdims=True)
    return x * lax.rsqrt(ms + EPS) * g


def _sigmoid(x):
    return 1.0 / (1.0 + jnp.exp(-x))


def _ffn_kernel(x_ref, g_ref, wg_ref, wu_ref, wd_ref, fg_ref, o_ref, h_ref, *, final_norm):
    j = pl.program_id(1)

    @pl.when(j == 0)
    def _():
        x = x_ref[...]
        h_ref[...] = _rms(x, g_ref[...]).astype(BF16)
        o_ref[...] = x

    h = h_ref[...]
    gate = jnp.dot(h, wg_ref[...], preferred_element_type=F32)
    up = jnp.dot(h, wu_ref[...], preferred_element_type=F32)
    a = (gate * _sigmoid(gate)) * (up * 0.5)
    o_ref[...] += jnp.dot(a.astype(BF16), wd_ref[...], preferred_element_type=F32)

    if final_norm:
        @pl.when(j == pl.num_programs(1) - 1)
        def _():
            o_ref[...] = _rms(o_ref[...], fg_ref[...])


def _ffn(x, g, wg, wu, wd, fg, *, tm, tf, final_norm):
    m = x.shape[0]
    grid = (m // tm, D_FF // tf)
    return pl.pallas_call(
        functools.partial(_ffn_kernel, final_norm=final_norm),
        grid=grid,
        in_specs=[
            pl.BlockSpec((tm, D_MODEL), lambda i, j: (i, 0)),
            pl.BlockSpec((1, D_MODEL), lambda i, j: (0, 0)),
            pl.BlockSpec((D_MODEL, tf), lambda i, j: (0, j)),
            pl.BlockSpec((D_MODEL, tf), lambda i, j: (0, j)),
            pl.BlockSpec((tf, D_MODEL), lambda i, j: (j, 0)),
            pl.BlockSpec((1, D_MODEL), lambda i, j: (0, 0)),
        ],
        out_specs=pl.BlockSpec((tm, D_MODEL), lambda i, j: (i, 0)),
        out_shape=jax.ShapeDtypeStruct((m, D_MODEL), F32),
        scratch_shapes=[pltpu.VMEM((tm, D_MODEL), BF16)],
        compiler_params=_params("parallel", "arbitrary"),
        name="ffn_final" if final_norm else "ffn",
    )(x, g, wg, wu, wd, fg)


def _qk_norm(x, g):
    outs = []
    for hh in range(x.shape[1] // LANES):
        xh = x[:, hh * LANES:(hh + 1) * LANES]
        lo = lax.broadcasted_iota(jnp.int32, xh.shape, 1) < QK_DIM
        x2 = xh * xh
        s_lo = jnp.sum(jnp.where(lo, x2, 0.0), axis=-1, keepdims=True)
        s_hi = jnp.sum(jnp.where(lo, 0.0, x2), axis=-1, keepdims=True)
        ms = jnp.where(lo, s_lo, s_hi) * (1.0 / QK_DIM)
        outs.append(xh * lax.rsqrt(ms + EPS) * g)
    return outs[0] if len(outs) == 1 else jnp.concatenate(outs, axis=1)


def _inproj_kernel(x_ref, g_ref, wq_ref, wk_ref, wvt_ref, wa_ref, wc_ref, qg_ref, kg_ref,
                   q_ref, k_ref, vt_ref, u_ref, h_ref):
    j = pl.program_id(1)

    @pl.when(j == 0)
    def _():
        h_ref[...] = _rms(x_ref[...], g_ref[...]).astype(BF16)

    h = h_ref[...]
    q = jnp.dot(h, wq_ref[...], preferred_element_type=F32)
    q_ref[...] = _qk_norm(q, qg_ref[...]).astype(BF16)
    k = jnp.dot(h, wk_ref[...], preferred_element_type=F32)
    k_ref[...] = _qk_norm(k, kg_ref[...]).astype(BF16)
    vt = lax.dot_general(wvt_ref[...], h, NT_DIMS, preferred_element_type=F32)
    vt_ref[...] = vt.astype(BF16)
    ca = jnp.dot(h, wa_ref[...], preferred_element_type=F32)
    cg = jnp.dot(h, wc_ref[...], preferred_element_type=F32)
    u_ref[...] = ca * _sigmoid(cg)


def _in_proj(x, g, w_in, wvt, qg, kg, *, tm, tn):
    m = x.shape[0]
    nb = ATTN_WIDTH // tn
    wspec = lambda c: pl.BlockSpec((D_MODEL, tn), lambda i, j, c=c: (0, c * nb + j))
    return pl.pallas_call(
        _inproj_kernel,
        grid=(m // tm, nb),
        in_specs=[
            pl.BlockSpec((tm, D_MODEL), lambda i, j: (i, 0)),
            pl.BlockSpec((1, D_MODEL), lambda i, j: (0, 0)),
            wspec(0), wspec(1),
            pl.BlockSpec((tn, D_MODEL), lambda i, j: (j, 0)),
            wspec(3), wspec(4),
            pl.BlockSpec((1, LANES), lambda i, j: (0, 0)),
            pl.BlockSpec((1, LANES), lambda i, j: (0, 0)),
        ],
        out_specs=[
            pl.BlockSpec((tm, tn), lambda i, j: (i, j)),
            pl.BlockSpec((tm, tn), lambda i, j: (i, j)),
            pl.BlockSpec((tn, tm), lambda i, j: (j, i)),
            pl.BlockSpec((tm, tn), lambda i, j: (i, j)),
        ],
        out_shape=[
            jax.ShapeDtypeStruct((m, ATTN_WIDTH), BF16),
            jax.ShapeDtypeStruct((m, ATTN_WIDTH), BF16),
            jax.ShapeDtypeStruct((ATTN_WIDTH, m), BF16),
            jax.ShapeDtypeStruct((m, CONV_WIDTH), F32),
        ],
        scratch_shapes=[pltpu.VMEM((tm, D_MODEL), BF16)],
        compiler_params=_params("parallel", "arbitrary"),
        name="in_proj",
    )(x, g, w_in, w_in, wvt, w_in, w_in, qg, kg)


def _attn_kernel(q_ref, k_ref, vt_ref, km_ref, vmt_ref, lq1_ref, lk1_ref, lq2_ref, lk2_ref, sg_ref,
                 o_ref, m_sc, l_sc, acc_sc, *, tq, tk):
    i = pl.program_id(2)
    q = q_ref[...]
    lane = lax.broadcasted_iota(jnp.int32, q.shape, 1)
    zero = jnp.zeros_like(q)
    qs = jnp.concatenate([jnp.where(lane < QK_DIM, q, zero),
                          jnp.where(lane < QK_DIM, zero, q)], axis=0)

    s = lax.dot_general(km_ref[...], qs, NT_DIMS, preferred_element_type=F32)
    m0 = jnp.max(s, axis=0, keepdims=True)
    p = jnp.exp(s - m0)
    m_sc[...] = m0
    l_sc[...] = jnp.sum(p, axis=0, keepdims=True)
    acc_sc[...] = jnp.dot(vmt_ref[...], p.astype(BF16), preferred_element_type=F32)

    def step(kt, vt, mask):
        s = lax.dot_general(kt, qs, NT_DIMS, preferred_element_type=F32)
        if mask is not None:
            s = jnp.where(mask, s, NEG_INF)
        m_old = m_sc[...]
        m_new = jnp.maximum(m_old, jnp.max(s, axis=0, keepdims=True))
        alpha = jnp.exp(m_old - m_new)
        p = jnp.exp(s - m_new)
        l_sc[...] = alpha * l_sc[...] + jnp.sum(p, axis=0, keepdims=True)
        acc_sc[...] = alpha * acc_sc[...] + jnp.dot(vt, p.astype(BF16), preferred_element_type=F32)
        m_sc[...] = m_new

    def body(jj, carry):
        off = pl.multiple_of(jj * tk, tk)
        step(k_ref[pl.ds(off, tk), :], vt_ref[:, pl.ds(off, tk)], None)
        return carry

    lax.fori_loop(0, i * (tq // tk), body, 0)

    for d in range(tq // tk):
        off = pl.multiple_of(i * tq + d * tk, tk)
        kchunk = (lax.broadcasted_iota(jnp.int32, (tk, 2 * tq), 0) + d * tk) // CHUNK
        col = lax.broadcasted_iota(jnp.int32, (tk, 2 * tq), 1)
        qchunk = jnp.where(col < tq, col, col - tq) // CHUNK
        step(k_ref[pl.ds(off, tk), :], vt_ref[:, pl.ds(off, tk)], kchunk <= qchunk)

    lam = (jnp.exp(jnp.sum(lq1_ref[...] * lk1_ref[...], axis=-1, keepdims=True))
           - jnp.exp(jnp.sum(lq2_ref[...] * lk2_ref[...], axis=-1, keepdims=True)) + LAM_INIT)
    o = acc_sc[...] * (1.0 / l_sc[...])
    od = o[:, :tq] - lam * o[:, tq:]
    ms = jnp.mean(od * od, axis=0, keepdims=True)
    on = od * lax.rsqrt(ms + EPS) * sg_ref[...] * (1.0 - LAM_INIT)
    o_ref[...] = on.T.astype(BF16)


def _attention(q, k, vt, km, vmt, lq1, lk1, lq2, lk2, sg, *, batch, tq, tk):
    nq = SEQ // tq
    lspec = pl.BlockSpec((1, QK_DIM), lambda b, h, i: (0, 0))
    return pl.pallas_call(
        functools.partial(_attn_kernel, tq=tq, tk=tk),
        grid=(batch, N_HEADS, nq),
        in_specs=[
            pl.BlockSpec((tq, HEAD_DIM), lambda b, h, i: (b * nq + i, h)),
            pl.BlockSpec((SEQ, HEAD_DIM), lambda b, h, i: (b, h)),
            pl.BlockSpec((HEAD_DIM, SEQ), lambda b, h, i: (h, b)),
            pl.BlockSpec((N_META, HEAD_DIM), lambda b, h, i: (0, h)),
            pl.BlockSpec((HEAD_DIM, N_META), lambda b, h, i: (h, 0)),
            lspec, lspec, lspec, lspec,
            pl.BlockSpec((HEAD_DIM, 1), lambda b, h, i: (0, 0)),
        ],
        out_specs=pl.BlockSpec((tq, HEAD_DIM), lambda b, h, i: (b * nq + i, h)),
        out_shape=jax.ShapeDtypeStruct((batch * SEQ, ATTN_WIDTH), BF16),
        scratch_shapes=[
            pltpu.VMEM((1, 2 * tq), F32),
            pltpu.VMEM((1, 2 * tq), F32),
            pltpu.VMEM((HEAD_DIM, 2 * tq), F32),
        ],
        compiler_params=_params("parallel", "parallel", "arbitrary"),
        name="diff_attn",
    )(q, k, vt, km, vmt, lq1, lk1, lq2, lk2, sg)


def _conv_kernel(u_ref, prev_ref, halo0_ref, w_ref, b_ref, g_ref, o_ref, win_sc, y_sc, *, tl, rc):
    i = pl.program_id(1)

    @pl.when(i == 0)
    def _():
        win_sc[0:HALO, :] = halo0_ref[...]

    @pl.when(i > 0)
    def _():
        win_sc[0:HALO, :] = prev_ref[...]

    win_sc[HALO:HALO + tl, :] = u_ref[...]

    first = HALO - (CONV_KERNEL - 1)
    for c in range(CONV_WIDTH // LANES):
        cs = slice(c * LANES, (c + 1) * LANES)
        for r in range(tl // rc):
            acc = jnp.broadcast_to(b_ref[:, cs], (rc, LANES))
            for t in range(CONV_KERNEL):
                lo = r * rc + first + t
                acc = acc + w_ref[t:t + 1, cs] * win_sc[lo:lo + rc, cs]
            y_sc[r * rc:(r + 1) * rc, cs] = acc

    yn = _rms(y_sc[...], g_ref[...])
    o_ref[...] = (yn * _sigmoid(yn)).astype(BF16)


def _conv(u, halo0, w, b, g, *, batch, tl, rc):
    nt = SEQ // tl
    hb = tl // HALO
    return pl.pallas_call(
        functools.partial(_conv_kernel, tl=tl, rc=rc),
        grid=(batch, nt),
        in_specs=[
            pl.BlockSpec((tl, CONV_WIDTH), lambda bb, i: (bb * nt + i, 0)),
            pl.BlockSpec((HALO, CONV_WIDTH), lambda bb, i: (jnp.maximum((bb * nt + i) * hb - 1, 0), 0)),
            pl.BlockSpec((HALO, CONV_WIDTH), lambda bb, i: (0, 0)),
            pl.BlockSpec((CONV_KERNEL, CONV_WIDTH), lambda bb, i: (0, 0)),
            pl.BlockSpec((1, CONV_WIDTH), lambda bb, i: (0, 0)),
            pl.BlockSpec((1, CONV_WIDTH), lambda bb, i: (0, 0)),
        ],
        out_specs=pl.BlockSpec((tl, CONV_WIDTH), lambda bb, i: (bb * nt + i, 0)),
        out_shape=jax.ShapeDtypeStruct((batch * SEQ, CONV_WIDTH), BF16),
        scratch_shapes=[
            pltpu.VMEM((HALO + tl, CONV_WIDTH), F32),
            pltpu.VMEM((tl, CONV_WIDTH), F32),
        ],
        compiler_params=_params("parallel", "arbitrary"),
        name="conv",
    )(u, u, halo0, w, b, g)


def _outproj_kernel(x_ref, a_ref, c_ref, wa_ref, wc_ref, o_ref):
    o_ref[...] = (x_ref[...]
                  + jnp.dot(a_ref[...], wa_ref[...], preferred_element_type=F32)
                  + jnp.dot(c_ref[...], wc_ref[...], preferred_element_type=F32))


def _out_proj(x, oa, oc, w_out, *, tm):
    m = x.shape[0]
    return pl.pallas_call(
        _outproj_kernel,
        grid=(m // tm,),
        in_specs=[
            pl.BlockSpec((tm, D_MODEL), lambda i: (i, 0)),
            pl.BlockSpec((tm, ATTN_WIDTH), lambda i: (i, 0)),
            pl.BlockSpec((tm, CONV_WIDTH), lambda i: (i, 0)),
            pl.BlockSpec((ATTN_WIDTH, D_MODEL), lambda i: (0, 0)),
            pl.BlockSpec((CONV_WIDTH, D_MODEL), lambda i: (1, 0)),
        ],
        out_specs=pl.BlockSpec((tm, D_MODEL), lambda i: (i, 0)),
        out_shape=jax.ShapeDtypeStruct((m, D_MODEL), F32),
        compiler_params=_params("parallel"),
        name="out_proj",
    )(x, oa, oc, w_out, w_out)


def kernel(x, meta_tokens, ffn1_norm_g, ffn1_w_gate, ffn1_w_up, ffn1_w_down, mix_norm_g, w_in, q_norm_g,
           k_norm_g, lambda_q1, lambda_k1, lambda_q2, lambda_k2, attn_subln_g, conv_w, conv_b, conv_norm_g,
           w_out, ffn2_norm_g, ffn2_w_gate, ffn2_w_up, ffn2_w_down, final_norm_g):
    batch = x.shape[0]
    xr = x.reshape(batch * SEQ, D_MODEL)

    w1g, w1u, w1d = ffn1_w_gate[0].astype(BF16), ffn1_w_up[0].astype(BF16), ffn1_w_down[0].astype(BF16)
    w2g, w2u, w2d = ffn2_w_gate[0].astype(BF16), ffn2_w_up[0].astype(BF16), ffn2_w_down[0].astype(BF16)
    win = w_in[0].astype(BF16)
    wvt = w_in[0][:, 2 * ATTN_WIDTH:3 * ATTN_WIDTH].T.astype(BF16)
    wout = w_out[0].astype(BF16)
    qg = jnp.tile(q_norm_g[0] * (QK_DIM ** -0.5), 2)[None, :]
    kg = jnp.tile(k_norm_g[0], 2)[None, :]
    sg = attn_subln_g[0][:, None]
    cw = conv_w[0].reshape(CONV_KERNEL, CONV_WIDTH)

    ffn = functools.partial(_ffn, tf=512)
    m1 = ffn(meta_tokens, ffn1_norm_g, w1g, w1u, w1d, final_norm_g, tm=N_META, final_norm=False)
    _, km, vmt, um = _in_proj(m1, mix_norm_g, win, wvt, qg, kg, tm=N_META, tn=512)
    halo0 = jnp.concatenate([jnp.zeros((HALO - N_META, CONV_WIDTH), F32), um], axis=0)

    x1 = ffn(xr, ffn1_norm_g, w1g, w1u, w1d, final_norm_g, tm=512, final_norm=False)
    q, k, vt, u = _in_proj(x1, mix_norm_g, win, wvt, qg, kg, tm=512, tn=512)
    oa = _attention(q, k, vt, km, vmt, lambda_q1, lambda_k1, lambda_q2, lambda_k2, sg,
                    batch=batch, tq=512, tk=256)
    oc = _conv(u, halo0, cw, conv_b, conv_norm_g, batch=batch, tl=256, rc=64)
    x2 = _out_proj(x1, oa, oc, wout, tm=512)
    y = ffn(x2, ffn2_norm_g, w2g, w2u, w2d, final_norm_g, tm=512, final_norm=True)
    return y.reshape(batch, SEQ, D_MODEL)
```

```python
import functools
import math

import jax
import jax.numpy as jnp
from jax import lax
from jax.experimental import pallas as pl
from jax.experimental.pallas import tpu as pltpu

F32 = jnp.float32
BF16 = jnp.bfloat16

D_MODEL = 2048
SEQ = 4096
N_META = 16
CHUNK = 64
ATTN_WIDTH = 1024
CONV_WIDTH = 1024
HEAD_DIM = 128
N_HEADS = ATTN_WIDTH // HEAD_DIM
QK_DIM = HEAD_DIM // 2
CONV_KERNEL = 31
D_FF = 5632
EPS = 1e-6
NEG_INF = -1e30
LAM_INIT = 0.8 - 0.6 * math.exp(-0.3 * 1)

LOG2E = math.log2(math.e)

LANES = 128
DENOM_ROWS = 16
HALO = 32
VMEM_LIMIT = 56 * 1024 * 1024

NT_DIMS = (((1,), (1,)), ((), ()))


def _params(*sem):
    return pltpu.CompilerParams(dimension_semantics=sem, vmem_limit_bytes=VMEM_LIMIT)


def _rms(x, g):
    ms = jnp.mean(x * x, axis=-1, keepdims=True)
    return x * lax.rsqrt(ms + EPS) * g


def _sigmoid(x):
    return 1.0 / (1.0 + jnp.exp(-x))


def _ffn_kernel(x_ref, g_ref, wg_ref, wu_ref, wd_ref, fg_ref, o_ref, h_ref, *, final_norm):
    j = pl.program_id(1)

    @pl.when(j == 0)
    def _():
        x = x_ref[...]
        h_ref[...] = _rms(x, g_ref[...]).astype(BF16)
        o_ref[...] = x

    h = h_ref[...]
    gate = jnp.dot(h, wg_ref[...], preferred_element_type=F32)
    up = jnp.dot(h, wu_ref[...], preferred_element_type=F32)
    a = (gate * _sigmoid(gate)) * (up * 0.5)
    o_ref[...] += jnp.dot(a.astype(BF16), wd_ref[...], preferred_element_type=F32)

    if final_norm:
        @pl.when(j == pl.num_programs(1) - 1)
        def _():
            o_ref[...] = _rms(o_ref[...], fg_ref[...])


def _ffn(x, g, wg, wu, wd, fg, *, tm, tf, final_norm):
    m = x.shape[0]
    grid = (m // tm, D_FF // tf)
    return pl.pallas_call(
        functools.partial(_ffn_kernel, final_norm=final_norm),
        grid=grid,
        in_specs=[
            pl.BlockSpec((tm, D_MODEL), lambda i, j: (i, 0)),
            pl.BlockSpec((1, D_MODEL), lambda i, j: (0, 0)),
            pl.BlockSpec((D_MODEL, tf), lambda i, j: (0, j)),
            pl.BlockSpec((D_MODEL, tf), lambda i, j: (0, j)),
            pl.BlockSpec((tf, D_MODEL), lambda i, j: (j, 0)),
            pl.BlockSpec((1, D_MODEL), lambda i, j: (0, 0)),
        ],
        out_specs=pl.BlockSpec((tm, D_MODEL), lambda i, j: (i, 0)),
        out_shape=jax.ShapeDtypeStruct((m, D_MODEL), F32),
        scratch_shapes=[pltpu.VMEM((tm, D_MODEL), BF16)],
        compiler_params=_params("parallel", "arbitrary"),
        name="ffn_final" if final_norm else "ffn",
    )(x, g, wg, wu, wd, fg)


def _qk_norm(x, g):
    outs = []
    for hh in range(x.shape[1] // LANES):
        xh = x[:, hh * LANES:(hh + 1) * LANES]
        lo = lax.broadcasted_iota(jnp.int32, xh.shape, 1) < QK_DIM
        x2 = xh * xh
        s_lo = jnp.sum(jnp.where(lo, x2, 0.0), axis=-1, keepdims=True)
        s_hi = jnp.sum(jnp.where(lo, 0.0, x2), axis=-1, keepdims=True)
        ms = jnp.where(lo, s_lo, s_hi) * (1.0 / QK_DIM)
        outs.append(xh * lax.rsqrt(ms + EPS) * g)
    return outs[0] if len(outs) == 1 else jnp.concatenate(outs, axis=1)


def _inproj_kernel(x_ref, g_ref, wq_ref, wk_ref, wvt_ref, wa_ref, wc_ref, qg_ref, kg_ref,
                   q_ref, k_ref, vt_ref, u_ref, h_ref):
    j = pl.program_id(1)

    @pl.when(j == 0)
    def _():
        h_ref[...] = _rms(x_ref[...], g_ref[...]).astype(BF16)

    h = h_ref[...]
    q = jnp.dot(h, wq_ref[...], preferred_element_type=F32)
    q_ref[...] = _qk_norm(q, qg_ref[...]).astype(BF16)
    k = jnp.dot(h, wk_ref[...], preferred_element_type=F32)
    k_ref[...] = _qk_norm(k, kg_ref[...]).astype(BF16)
    vt = lax.dot_general(wvt_ref[...], h, NT_DIMS, preferred_element_type=F32)
    vt_ref[...] = vt.astype(BF16)
    ca = jnp.dot(h, wa_ref[...], preferred_element_type=F32)
    cg = jnp.dot(h, wc_ref[...], preferred_element_type=F32)
    u_ref[...] = ca * _sigmoid(cg)


def _in_proj(x, g, w_in, wvt, qg, kg, *, tm, tn):
    m = x.shape[0]
    nb = ATTN_WIDTH // tn
    wspec = lambda c: pl.BlockSpec((D_MODEL, tn), lambda i, j, c=c: (0, c * nb + j))
    return pl.pallas_call(
        _inproj_kernel,
        grid=(m // tm, nb),
        in_specs=[
            pl.BlockSpec((tm, D_MODEL), lambda i, j: (i, 0)),
            pl.BlockSpec((1, D_MODEL), lambda i, j: (0, 0)),
            wspec(0), wspec(1),
            pl.BlockSpec((tn, D_MODEL), lambda i, j: (j, 0)),
            wspec(3), wspec(4),
            pl.BlockSpec((1, LANES), lambda i, j: (0, 0)),
            pl.BlockSpec((1, LANES), lambda i, j: (0, 0)),
        ],
        out_specs=[
            pl.BlockSpec((tm, tn), lambda i, j: (i, j)),
            pl.BlockSpec((tm, tn), lambda i, j: (i, j)),
            pl.BlockSpec((tn, tm), lambda i, j: (j, i)),
            pl.BlockSpec((tm, tn), lambda i, j: (i, j)),
        ],
        out_shape=[
            jax.ShapeDtypeStruct((m, ATTN_WIDTH), BF16),
            jax.ShapeDtypeStruct((m, ATTN_WIDTH), BF16),
            jax.ShapeDtypeStruct((ATTN_WIDTH, m), BF16),
            jax.ShapeDtypeStruct((m, CONV_WIDTH), F32),
        ],
        scratch_shapes=[pltpu.VMEM((tm, D_MODEL), BF16)],
        compiler_params=_params("parallel", "arbitrary"),
        name="in_proj",
    )(x, g, w_in, w_in, wvt, w_in, w_in, qg, kg)


def _attn_kernel(q_ref, k_ref, vt_ref, km_ref, vmt_ref, lq1_ref, lk1_ref, lq2_ref, lk2_ref, sg_ref,
                 o_ref, m_sc, acc_sc, s0_sc, s1_sc, *, tq, tk, cw):
    i = pl.program_id(2)
    q = q_ref[...]
    lane = lax.broadcasted_iota(jnp.int32, q.shape, 1)
    zero = jnp.zeros_like(q)
    qs = jnp.concatenate([jnp.where(lane < QK_DIM, q, zero),
                          jnp.where(lane < QK_DIM, zero, q)], axis=0)

    def scores(blk, s_ref):
        off = pl.multiple_of(blk * tk, tk)
        s_ref[...] = lax.dot_general(k_ref[pl.ds(off, tk), :], qs, NT_DIMS, preferred_element_type=F32)

    def with_ones(vt):
        return jnp.concatenate([vt, jnp.ones((DENOM_ROWS, vt.shape[1]), BF16)], axis=0)

    def softmax_pv(blk, s_ref, diag):
        off = pl.multiple_of(blk * tk, tk)
        vt = with_ones(vt_ref[:, pl.ds(off, tk)])
        for c in range(2 * tq // cw):
            cs = slice(c * cw, (c + 1) * cw)
            s = s_ref[:, cs]
            if diag is not None:
                k_lo, k_hi = diag * tk // CHUNK, ((diag + 1) * tk - 1) // CHUNK
                q0 = (c * cw) % tq
                q_lo, q_hi = q0 // CHUNK, (q0 + cw - 1) // CHUNK
                if k_lo > q_hi:
                    continue
                if k_hi > q_lo:
                    kchunk = (lax.broadcasted_iota(jnp.int32, (tk, cw), 0) + diag * tk) // CHUNK
                    qchunk = (lax.broadcasted_iota(jnp.int32, (tk, cw), 1) + q0) // CHUNK
                    s = jnp.where(kchunk <= qchunk, s, NEG_INF)
            m_old = m_sc[:, cs]
            m_new = jnp.maximum(m_old, jnp.max(s, axis=0, keepdims=True))
            alpha = jnp.exp2(m_old - m_new)
            p = jnp.exp2(s - m_new)
            acc_sc[:, cs] = alpha * acc_sc[:, cs] + jnp.dot(vt, p.astype(BF16), preferred_element_type=F32)
            m_sc[:, cs] = m_new

    scores(0, s0_sc)

    s = lax.dot_general(km_ref[...], qs, NT_DIMS, preferred_element_type=F32)
    m0 = jnp.max(s, axis=0, keepdims=True)
    p = jnp.exp2(s - m0)
    m_sc[...] = m0
    acc_sc[...] = jnp.dot(with_ones(vmt_ref[...]), p.astype(BF16), preferred_element_type=F32)

    def pair(jp, carry):
        scores(2 * jp + 1, s1_sc)
        softmax_pv(2 * jp, s0_sc, None)
        scores(2 * jp + 2, s0_sc)
        softmax_pv(2 * jp + 1, s1_sc, None)
        return carry

    lax.fori_loop(0, i, pair, 0)

    scores(2 * i + 1, s1_sc)
    softmax_pv(2 * i, s0_sc, 0)
    softmax_pv(2 * i + 1, s1_sc, 1)

    lam = (jnp.exp(jnp.sum(lq1_ref[...] * lk1_ref[...], axis=-1, keepdims=True))
           - jnp.exp(jnp.sum(lq2_ref[...] * lk2_ref[...], axis=-1, keepdims=True)) + LAM_INIT)
    acc = acc_sc[...]
    o = acc[:HEAD_DIM] * (1.0 / acc[HEAD_DIM:HEAD_DIM + 1])
    od = o[:, :tq] - lam * o[:, tq:]
    ms = jnp.mean(od * od, axis=0, keepdims=True)
    on = od * lax.rsqrt(ms + EPS) * sg_ref[...] * (1.0 - LAM_INIT)
    o_ref[...] = on.T.astype(BF16)


def _attention(q, k, vt, km, vmt, lq1, lk1, lq2, lk2, sg, *, batch, tq, cw):
    tk = tq // 2
    nq = SEQ // tq
    lspec = pl.BlockSpec((1, QK_DIM), lambda b, h, i: (0, 0))
    return pl.pallas_call(
        functools.partial(_attn_kernel, tq=tq, tk=tk, cw=cw),
        grid=(batch, N_HEADS, nq),
        in_specs=[
            pl.BlockSpec((tq, HEAD_DIM), lambda b, h, i: (b * nq + i, h)),
            pl.BlockSpec((SEQ, HEAD_DIM), lambda b, h, i: (b, h)),
            pl.BlockSpec((HEAD_DIM, SEQ), lambda b, h, i: (h, b)),
            pl.BlockSpec((N_META, HEAD_DIM), lambda b, h, i: (0, h)),
            pl.BlockSpec((HEAD_DIM, N_META), lambda b, h, i: (h, 0)),
            lspec, lspec, lspec, lspec,
            pl.BlockSpec((HEAD_DIM, 1), lambda b, h, i: (0, 0)),
        ],
        out_specs=pl.BlockSpec((tq, HEAD_DIM), lambda b, h, i: (b * nq + i, h)),
        out_shape=jax.ShapeDtypeStruct((batch * SEQ, ATTN_WIDTH), BF16),
        scratch_shapes=[
            pltpu.VMEM((1, 2 * tq), F32),
            pltpu.VMEM((HEAD_DIM + DENOM_ROWS, 2 * tq), F32),
            pltpu.VMEM((tk, 2 * tq), F32),
            pltpu.VMEM((tk, 2 * tq), F32),
        ],
        compiler_params=_params("parallel", "parallel", "arbitrary"),
        name="diff_attn",
    )(q, k, vt, km, vmt, lq1, lk1, lq2, lk2, sg)


def _conv_kernel(u_ref, prev_ref, halo0_ref, w_ref, b_ref, g_ref, o_ref, win_sc, y_sc, *, tl, rc):
    i = pl.program_id(1)

    @pl.when(i == 0)
    def _():
        win_sc[0:HALO, :] = halo0_ref[...]

    @pl.when(i > 0)
    def _():
        win_sc[0:HALO, :] = prev_ref[...]

    win_sc[HALO:HALO + tl, :] = u_ref[...]

    first = HALO - (CONV_KERNEL - 1)
    for c in range(CONV_WIDTH // LANES):
        cs = slice(c * LANES, (c + 1) * LANES)
        for r in range(tl // rc):
            acc = jnp.broadcast_to(b_ref[:, cs], (rc, LANES))
            for t in range(CONV_KERNEL):
                lo = r * rc + first + t
                acc = acc + w_ref[t:t + 1, cs] * win_sc[lo:lo + rc, cs]
            y_sc[r * rc:(r + 1) * rc, cs] = acc

    yn = _rms(y_sc[...], g_ref[...])
    o_ref[...] = (yn * _sigmoid(yn)).astype(BF16)


def _conv(u, halo0, w, b, g, *, batch, tl, rc):
    nt = SEQ // tl
    hb = tl // HALO
    return pl.pallas_call(
        functools.partial(_conv_kernel, tl=tl, rc=rc),
        grid=(batch, nt),
        in_specs=[
            pl.BlockSpec((tl, CONV_WIDTH), lambda bb, i: (bb * nt + i, 0)),
            pl.BlockSpec((HALO, CONV_WIDTH), lambda bb, i: (jnp.maximum((bb * nt + i) * hb - 1, 0), 0)),
            pl.BlockSpec((HALO, CONV_WIDTH), lambda bb, i: (0, 0)),
            pl.BlockSpec((CONV_KERNEL, CONV_WIDTH), lambda bb, i: (0, 0)),
            pl.BlockSpec((1, CONV_WIDTH), lambda bb, i: (0, 0)),
            pl.BlockSpec((1, CONV_WIDTH), lambda bb, i: (0, 0)),
        ],
        out_specs=pl.BlockSpec((tl, CONV_WIDTH), lambda bb, i: (bb * nt + i, 0)),
        out_shape=jax.ShapeDtypeStruct((batch * SEQ, CONV_WIDTH), BF16),
        scratch_shapes=[
            pltpu.VMEM((HALO + tl, CONV_WIDTH), F32),
            pltpu.VMEM((tl, CONV_WIDTH), F32),
        ],
        compiler_params=_params("parallel", "arbitrary"),
        name="conv",
    )(u, u, halo0, w, b, g)


def _outproj_kernel(x_ref, a_ref, c_ref, wa_ref, wc_ref, o_ref):
    o_ref[...] = (x_ref[...]
                  + jnp.dot(a_ref[...], wa_ref[...], preferred_element_type=F32)
                  + jnp.dot(c_ref[...], wc_ref[...], preferred_element_type=F32))


def _out_proj(x, oa, oc, w_out, *, tm):
    m = x.shape[0]
    return pl.pallas_call(
        _outproj_kernel,
        grid=(m // tm,),
        in_specs=[
            pl.BlockSpec((tm, D_MODEL), lambda i: (i, 0)),
            pl.BlockSpec((tm, ATTN_WIDTH), lambda i: (i, 0)),
            pl.BlockSpec((tm, CONV_WIDTH), lambda i: (i, 0)),
            pl.BlockSpec((ATTN_WIDTH, D_MODEL), lambda i: (0, 0)),
            pl.BlockSpec((CONV_WIDTH, D_MODEL), lambda i: (1, 0)),
        ],
        out_specs=pl.BlockSpec((tm, D_MODEL), lambda i: (i, 0)),
        out_shape=jax.ShapeDtypeStruct((m, D_MODEL), F32),
        compiler_params=_params("parallel"),
        name="out_proj",
    )(x, oa, oc, w_out, w_out)


def kernel(x, meta_tokens, ffn1_norm_g, ffn1_w_gate, ffn1_w_up, ffn1_w_down, mix_norm_g, w_in, q_norm_g,
           k_norm_g, lambda_q1, lambda_k1, lambda_q2, lambda_k2, attn_subln_g, conv_w, conv_b, conv_norm_g,
           w_out, ffn2_norm_g, ffn2_w_gate, ffn2_w_up, ffn2_w_down, final_norm_g):
    batch = x.shape[0]
    xr = x.reshape(batch * SEQ, D_MODEL)

    w1g, w1u, w1d = ffn1_w_gate[0].astype(BF16), ffn1_w_up[0].astype(BF16), ffn1_w_down[0].astype(BF16)
    w2g, w2u, w2d = ffn2_w_gate[0].astype(BF16), ffn2_w_up[0].astype(BF16), ffn2_w_down[0].astype(BF16)
    win = w_in[0].astype(BF16)
    wvt = w_in[0][:, 2 * ATTN_WIDTH:3 * ATTN_WIDTH].T.astype(BF16)
    wout = w_out[0].astype(BF16)
    qg = jnp.tile(q_norm_g[0] * (QK_DIM ** -0.5 * LOG2E), 2)[None, :]
    kg = jnp.tile(k_norm_g[0], 2)[None, :]
    sg = attn_subln_g[0][:, None]
    cw = conv_w[0].reshape(CONV_KERNEL, CONV_WIDTH)

    ffn = functools.partial(_ffn, tf=512)
    m1 = ffn(meta_tokens, ffn1_norm_g, w1g, w1u, w1d, final_norm_g, tm=N_META, final_norm=False)
    _, km, vmt, um = _in_proj(m1, mix_norm_g, win, wvt, qg, kg, tm=N_META, tn=512)
    halo0 = jnp.concatenate([jnp.zeros((HALO - N_META, CONV_WIDTH), F32), um], axis=0)

    x1 = ffn(xr, ffn1_norm_g, w1g, w1u, w1d, final_norm_g, tm=512, final_norm=False)
    q, k, vt, u = _in_proj(x1, mix_norm_g, win, wvt, qg, kg, tm=512, tn=512)
    oa = _attention(q, k, vt, km, vmt, lambda_q1, lambda_k1, lambda_q2, lambda_k2, sg,
                    batch=batch, tq=512, cw=256)
    oc = _conv(u, halo0, cw, conv_b, conv_norm_g, batch=batch, tl=256, rc=64)
    x2 = _out_proj(x1, oa, oc, wout, tm=512)
    y = ffn(x2, ffn2_norm_g, w2g, w2u, w2d, final_norm_g, tm=512, final_norm=True)
    return y.reshape(batch, SEQ, D_MODEL)
```

```python
import functools
import math

import jax
import jax.numpy as jnp
from jax import lax
from jax.experimental import pallas as pl
from jax.experimental.pallas import tpu as pltpu

F32 = jnp.float32
BF16 = jnp.bfloat16

D_MODEL = 2048
SEQ = 4096
N_META = 16
CHUNK = 64
ATTN_WIDTH = 1024
CONV_WIDTH = 1024
HEAD_DIM = 128
N_HEADS = ATTN_WIDTH // HEAD_DIM
QK_DIM = HEAD_DIM // 2
CONV_KERNEL = 31
D_FF = 5632
EPS = 1e-6
NEG_INF = -1e30
LAM_INIT = 0.8 - 0.6 * math.exp(-0.3 * 1)

LOG2E = math.log2(math.e)

LANES = 128
SUBLANES = 8
DENOM_ROWS = 16
HALO = 32
VMEM_LIMIT = 56 * 1024 * 1024

NT_DIMS = (((1,), (1,)), ((), ()))


def _params(*sem):
    return pltpu.CompilerParams(dimension_semantics=sem, vmem_limit_bytes=VMEM_LIMIT)


def _rms(x, g):
    ms = jnp.mean(x * x, axis=-1, keepdims=True)
    return x * lax.rsqrt(ms + EPS) * g


def _sigmoid(x):
    return 1.0 / (1.0 + jnp.exp(-x))


def _ffn_kernel(x_ref, g_ref, wg_ref, wu_ref, wd_ref, fg_ref, o_ref, h_ref, *, final_norm):
    j = pl.program_id(1)

    @pl.when(j == 0)
    def _():
        x = x_ref[...]
        h_ref[...] = _rms(x, g_ref[...]).astype(BF16)
        o_ref[...] = x

    h = h_ref[...]
    gate = jnp.dot(h, wg_ref[...], preferred_element_type=F32)
    up = jnp.dot(h, wu_ref[...], preferred_element_type=F32)
    a = (gate * _sigmoid(gate)) * (up * 0.5)
    o_ref[...] += jnp.dot(a.astype(BF16), wd_ref[...], preferred_element_type=F32)

    if final_norm:
        @pl.when(j == pl.num_programs(1) - 1)
        def _():
            o_ref[...] = _rms(o_ref[...], fg_ref[...])


def _ffn(x, g, wg, wu, wd, fg, *, tm, tf, final_norm):
    m = x.shape[0]
    grid = (m // tm, D_FF // tf)
    return pl.pallas_call(
        functools.partial(_ffn_kernel, final_norm=final_norm),
        grid=grid,
        in_specs=[
            pl.BlockSpec((tm, D_MODEL), lambda i, j: (i, 0)),
            pl.BlockSpec((1, D_MODEL), lambda i, j: (0, 0)),
            pl.BlockSpec((D_MODEL, tf), lambda i, j: (0, j)),
            pl.BlockSpec((D_MODEL, tf), lambda i, j: (0, j)),
            pl.BlockSpec((tf, D_MODEL), lambda i, j: (j, 0)),
            pl.BlockSpec((1, D_MODEL), lambda i, j: (0, 0)),
        ],
        out_specs=pl.BlockSpec((tm, D_MODEL), lambda i, j: (i, 0)),
        out_shape=jax.ShapeDtypeStruct((m, D_MODEL), F32),
        scratch_shapes=[pltpu.VMEM((tm, D_MODEL), BF16)],
        compiler_params=_params("parallel", "arbitrary"),
        name="ffn_final" if final_norm else "ffn",
    )(x, g, wg, wu, wd, fg)


def _qk_norm(x, g):
    outs = []
    for hh in range(x.shape[1] // LANES):
        xh = x[:, hh * LANES:(hh + 1) * LANES]
        lo = lax.broadcasted_iota(jnp.int32, xh.shape, 1) < QK_DIM
        x2 = xh * xh
        s_lo = jnp.sum(jnp.where(lo, x2, 0.0), axis=-1, keepdims=True)
        s_hi = jnp.sum(jnp.where(lo, 0.0, x2), axis=-1, keepdims=True)
        ms = jnp.where(lo, s_lo, s_hi) * (1.0 / QK_DIM)
        outs.append(xh * lax.rsqrt(ms + EPS) * g)
    return outs[0] if len(outs) == 1 else jnp.concatenate(outs, axis=1)


def _inproj_kernel(x_ref, g_ref, wq_ref, wk_ref, wvt_ref, wa_ref, wc_ref, qg_ref, kg_ref,
                   q_ref, k_ref, vt_ref, u_ref, h_ref):
    j = pl.program_id(1)

    @pl.when(j == 0)
    def _():
        h_ref[...] = _rms(x_ref[...], g_ref[...]).astype(BF16)

    h = h_ref[...]
    q = jnp.dot(h, wq_ref[...], preferred_element_type=F32)
    q_ref[...] = _qk_norm(q, qg_ref[...]).astype(BF16)
    k = jnp.dot(h, wk_ref[...], preferred_element_type=F32)
    k_ref[...] = _qk_norm(k, kg_ref[...]).astype(BF16)
    vt = lax.dot_general(wvt_ref[...], h, NT_DIMS, preferred_element_type=F32)
    vt_ref[...] = vt.astype(BF16)
    ca = jnp.dot(h, wa_ref[...], preferred_element_type=F32)
    cg = jnp.dot(h, wc_ref[...], preferred_element_type=F32)
    u_ref[...] = ca * _sigmoid(cg)


def _in_proj(x, g, w_in, wvt, qg, kg, *, tm, tn):
    m = x.shape[0]
    nb = ATTN_WIDTH // tn
    wspec = lambda c: pl.BlockSpec((D_MODEL, tn), lambda i, j, c=c: (0, c * nb + j))
    return pl.pallas_call(
        _inproj_kernel,
        grid=(m // tm, nb),
        in_specs=[
            pl.BlockSpec((tm, D_MODEL), lambda i, j: (i, 0)),
            pl.BlockSpec((1, D_MODEL), lambda i, j: (0, 0)),
            wspec(0), wspec(1),
            pl.BlockSpec((tn, D_MODEL), lambda i, j: (j, 0)),
            wspec(3), wspec(4),
            pl.BlockSpec((1, LANES), lambda i, j: (0, 0)),
            pl.BlockSpec((1, LANES), lambda i, j: (0, 0)),
        ],
        out_specs=[
            pl.BlockSpec((tm, tn), lambda i, j: (i, j)),
            pl.BlockSpec((tm, tn), lambda i, j: (i, j)),
            pl.BlockSpec((tn, tm), lambda i, j: (j, i)),
            pl.BlockSpec((tm, tn), lambda i, j: (i, j)),
        ],
        out_shape=[
            jax.ShapeDtypeStruct((m, ATTN_WIDTH), BF16),
            jax.ShapeDtypeStruct((m, ATTN_WIDTH), BF16),
            jax.ShapeDtypeStruct((ATTN_WIDTH, m), BF16),
            jax.ShapeDtypeStruct((m, CONV_WIDTH), F32),
        ],
        scratch_shapes=[pltpu.VMEM((tm, D_MODEL), BF16)],
        compiler_params=_params("parallel", "arbitrary"),
        name="in_proj",
    )(x, g, w_in, w_in, wvt, w_in, w_in, qg, kg)


def _attn_kernel(q_ref, k_ref, vt_ref, km_ref, vmt_ref, lq1_ref, lk1_ref, lq2_ref, lk2_ref, sg_ref,
                 o_ref, m_sc, acc_sc, s0_sc, s1_sc, *, tq, tk, cw):
    i = pl.program_id(2)
    q = q_ref[...]
    lane = lax.broadcasted_iota(jnp.int32, q.shape, 1)
    zero = jnp.zeros_like(q)
    qs = jnp.concatenate([jnp.where(lane < QK_DIM, q, zero),
                          jnp.where(lane < QK_DIM, zero, q)], axis=0)

    def scores(blk, s_ref):
        off = pl.multiple_of(blk * tk, tk)
        s_ref[...] = lax.dot_general(k_ref[pl.ds(off, tk), :], qs, NT_DIMS, preferred_element_type=F32)

    def with_ones(vt):
        return jnp.concatenate([vt, jnp.ones((DENOM_ROWS, vt.shape[1]), BF16)], axis=0)

    def online_update(s, vt, m_old, acc_old):
        m_new = jnp.maximum(m_old, jnp.max(s, axis=0, keepdims=True))
        alpha = jnp.exp2(m_old - m_new)
        p = jnp.exp2(s - m_new)
        return m_new, alpha * acc_old + jnp.dot(with_ones(vt), p.astype(BF16), preferred_element_type=F32)

    def softmax_pv(blk, s_ref, diag):
        off = pl.multiple_of(blk * tk, tk)
        for c in range(2 * tq // cw):
            cs = slice(c * cw, (c + 1) * cw)
            nk, masked = tk, False
            if diag is not None:
                q0 = (c * cw) % tq
                nk = min(tk, q0 + cw - diag * tk)
                if nk <= 0:
                    continue
                masked = (diag * tk + nk - 1) // CHUNK > q0 // CHUNK
            s = s_ref[0:nk, cs]
            if masked:
                kchunk = (lax.broadcasted_iota(jnp.int32, (nk, cw), 0) + diag * tk) // CHUNK
                qchunk = (lax.broadcasted_iota(jnp.int32, (nk, cw), 1) + q0) // CHUNK
                s = jnp.where(kchunk <= qchunk, s, NEG_INF)
            m_sc[:, cs], acc_sc[:, cs] = online_update(s, vt_ref[:, pl.ds(off, nk)], m_sc[:, cs], acc_sc[:, cs])

    scores(0, s0_sc)
    m_sc[...] = jnp.full(m_sc.shape, NEG_INF, F32)
    acc_sc[...] = jnp.zeros(acc_sc.shape, F32)

    def pair(jp, carry):
        scores(2 * jp + 1, s1_sc)
        softmax_pv(2 * jp, s0_sc, None)
        scores(2 * jp + 2, s0_sc)
        softmax_pv(2 * jp + 1, s1_sc, None)
        return carry

    lax.fori_loop(0, i, pair, 0)

    scores(2 * i + 1, s1_sc)
    softmax_pv(2 * i, s0_sc, 0)
    softmax_pv(2 * i + 1, s1_sc, 1)

    lam = (jnp.exp(jnp.sum(lq1_ref[...] * lk1_ref[...], axis=-1, keepdims=True))
           - jnp.exp(jnp.sum(lq2_ref[...] * lk2_ref[...], axis=-1, keepdims=True)) + LAM_INIT)
    s = lax.dot_general(km_ref[...], qs, NT_DIMS, preferred_element_type=F32)
    _, acc = online_update(s, vmt_ref[...], m_sc[...], acc_sc[...])
    o = acc[:HEAD_DIM] * (1.0 / acc[HEAD_DIM:HEAD_DIM + 1])
    od = o[:, :tq] - lam * o[:, tq:]
    ms = jnp.mean(od * od, axis=0, keepdims=True)
    on = od * lax.rsqrt(ms + EPS) * sg_ref[...] * (1.0 - LAM_INIT)
    o_ref[...] = on.T.astype(BF16)


def _attention(q, k, vt, km, vmt, lq1, lk1, lq2, lk2, sg, *, batch, tq, cw):
    tk = tq // 2
    assert tk % cw == 0 and cw % CHUNK == 0
    nq = SEQ // tq
    lspec = pl.BlockSpec((1, QK_DIM), lambda b, h, i: (0, 0))
    return pl.pallas_call(
        functools.partial(_attn_kernel, tq=tq, tk=tk, cw=cw),
        grid=(batch, N_HEADS, nq),
        in_specs=[
            pl.BlockSpec((tq, HEAD_DIM), lambda b, h, i: (b * nq + i, h)),
            pl.BlockSpec((SEQ, HEAD_DIM), lambda b, h, i: (b, h)),
            pl.BlockSpec((HEAD_DIM, SEQ), lambda b, h, i: (h, b)),
            pl.BlockSpec((N_META, HEAD_DIM), lambda b, h, i: (0, h)),
            pl.BlockSpec((HEAD_DIM, N_META), lambda b, h, i: (h, 0)),
            lspec, lspec, lspec, lspec,
            pl.BlockSpec((HEAD_DIM, 1), lambda b, h, i: (0, 0)),
        ],
        out_specs=pl.BlockSpec((tq, HEAD_DIM), lambda b, h, i: (b * nq + i, h)),
        out_shape=jax.ShapeDtypeStruct((batch * SEQ, ATTN_WIDTH), BF16),
        scratch_shapes=[
            pltpu.VMEM((1, 2 * tq), F32),
            pltpu.VMEM((HEAD_DIM + DENOM_ROWS, 2 * tq), F32),
            pltpu.VMEM((tk, 2 * tq), F32),
            pltpu.VMEM((tk, 2 * tq), F32),
        ],
        compiler_params=_params("parallel", "parallel", "arbitrary"),
        name="diff_attn",
    )(q, k, vt, km, vmt, lq1, lk1, lq2, lk2, sg)


def _conv_kernel(u_ref, prev_ref, halo0_ref, w_ref, b_ref, g_ref, o_ref, win_sc, shift_sc, y_sc, *, tl, rc, rb):
    i = pl.program_id(1)

    @pl.when(i == 0)
    def _():
        win_sc[0:HALO, :] = halo0_ref[...]

    @pl.when(i > 0)
    def _():
        win_sc[0:HALO, :] = prev_ref[...]

    win_sc[HALO:HALO + tl, :] = u_ref[...]

    nsh = HALO + tl - SUBLANES
    for s in range(1, SUBLANES):
        shift_sc[s - 1] = win_sc[s:s + nsh, :]

    first = HALO - (CONV_KERNEL - 1)
    @pl.loop(0, CONV_WIDTH // LANES)
    def _(c):
        cs = pl.ds(pl.multiple_of(c * LANES, LANES), LANES)
        for r0 in range(0, tl // rc, rb):
            accs = [jnp.broadcast_to(b_ref[:, cs], (rc, LANES))] * rb
            for t in range(CONV_KERNEL):
                wt = w_ref[t:t + 1, cs]
                for k in range(rb):
                    lo = (r0 + k) * rc + first + t
                    s, al = lo % SUBLANES, lo - lo % SUBLANES
                    tap = win_sc[al:al + rc, cs] if s == 0 else shift_sc[s - 1, al:al + rc, cs]
                    accs[k] = accs[k] + wt * tap
            for k in range(rb):
                y_sc[(r0 + k) * rc:(r0 + k + 1) * rc, cs] = accs[k]

    yn = _rms(y_sc[...], g_ref[...])
    o_ref[...] = (yn * _sigmoid(yn)).astype(BF16)


def _conv(u, halo0, w, b, g, *, batch, tl, rc, rb):
    nt = SEQ // tl
    hb = tl // HALO
    return pl.pallas_call(
        functools.partial(_conv_kernel, tl=tl, rc=rc, rb=rb),
        grid=(batch, nt),
        in_specs=[
            pl.BlockSpec((tl, CONV_WIDTH), lambda bb, i: (bb * nt + i, 0)),
            pl.BlockSpec((HALO, CONV_WIDTH), lambda bb, i: (jnp.maximum((bb * nt + i) * hb - 1, 0), 0)),
            pl.BlockSpec((HALO, CONV_WIDTH), lambda bb, i: (0, 0)),
            pl.BlockSpec((CONV_KERNEL, CONV_WIDTH), lambda bb, i: (0, 0)),
            pl.BlockSpec((1, CONV_WIDTH), lambda bb, i: (0, 0)),
            pl.BlockSpec((1, CONV_WIDTH), lambda bb, i: (0, 0)),
        ],
        out_specs=pl.BlockSpec((tl, CONV_WIDTH), lambda bb, i: (bb * nt + i, 0)),
        out_shape=jax.ShapeDtypeStruct((batch * SEQ, CONV_WIDTH), BF16),
        scratch_shapes=[
            pltpu.VMEM((HALO + tl, CONV_WIDTH), F32),
            pltpu.VMEM((SUBLANES - 1, HALO + tl - SUBLANES, CONV_WIDTH), F32),
            pltpu.VMEM((tl, CONV_WIDTH), F32),
        ],
        compiler_params=_params("parallel", "arbitrary"),
        name="conv",
    )(u, u, halo0, w, b, g)


def _outproj_kernel(x_ref, a_ref, c_ref, wa_ref, wc_ref, o_ref):
    o_ref[...] = (x_ref[...]
                  + jnp.dot(a_ref[...], wa_ref[...], preferred_element_type=F32)
                  + jnp.dot(c_ref[...], wc_ref[...], preferred_element_type=F32))


def _out_proj(x, oa, oc, w_out, *, tm):
    m = x.shape[0]
    return pl.pallas_call(
        _outproj_kernel,
        grid=(m // tm,),
        in_specs=[
            pl.BlockSpec((tm, D_MODEL), lambda i: (i, 0)),
            pl.BlockSpec((tm, ATTN_WIDTH), lambda i: (i, 0)),
            pl.BlockSpec((tm, CONV_WIDTH), lambda i: (i, 0)),
            pl.BlockSpec((ATTN_WIDTH, D_MODEL), lambda i: (0, 0)),
            pl.BlockSpec((CONV_WIDTH, D_MODEL), lambda i: (1, 0)),
        ],
        out_specs=pl.BlockSpec((tm, D_MODEL), lambda i: (i, 0)),
        out_shape=jax.ShapeDtypeStruct((m, D_MODEL), F32),
        compiler_params=_params("parallel"),
        name="out_proj",
    )(x, oa, oc, w_out, w_out)


def kernel(x, meta_tokens, ffn1_norm_g, ffn1_w_gate, ffn1_w_up, ffn1_w_down, mix_norm_g, w_in, q_norm_g,
           k_norm_g, lambda_q1, lambda_k1, lambda_q2, lambda_k2, attn_subln_g, conv_w, conv_b, conv_norm_g,
           w_out, ffn2_norm_g, ffn2_w_gate, ffn2_w_up, ffn2_w_down, final_norm_g):
    batch = x.shape[0]
    xr = x.reshape(batch * SEQ, D_MODEL)

    w1g, w1u, w1d = ffn1_w_gate[0].astype(BF16), ffn1_w_up[0].astype(BF16), ffn1_w_down[0].astype(BF16)
    w2g, w2u, w2d = ffn2_w_gate[0].astype(BF16), ffn2_w_up[0].astype(BF16), ffn2_w_down[0].astype(BF16)
    win = w_in[0].astype(BF16)
    wvt = w_in[0][:, 2 * ATTN_WIDTH:3 * ATTN_WIDTH].T.astype(BF16)
    wout = w_out[0].astype(BF16)
    qg = jnp.tile(q_norm_g[0] * (QK_DIM ** -0.5 * LOG2E), 2)[None, :]
    kg = jnp.tile(k_norm_g[0], 2)[None, :]
    sg = attn_subln_g[0][:, None]
    cw = conv_w[0].reshape(CONV_KERNEL, CONV_WIDTH)

    ffn = functools.partial(_ffn, tf=512)
    m1 = ffn(meta_tokens, ffn1_norm_g, w1g, w1u, w1d, final_norm_g, tm=N_META, final_norm=False)
    _, km, vmt, um = _in_proj(m1, mix_norm_g, win, wvt, qg, kg, tm=N_META, tn=512)
    halo0 = jnp.concatenate([jnp.zeros((HALO - N_META, CONV_WIDTH), F32), um], axis=0)

    x1 = ffn(xr, ffn1_norm_g, w1g, w1u, w1d, final_norm_g, tm=512, final_norm=False)
    q, k, vt, u = _in_proj(x1, mix_norm_g, win, wvt, qg, kg, tm=512, tn=512)
    oa = _attention(q, k, vt, km, vmt, lambda_q1, lambda_k1, lambda_q2, lambda_k2, sg,
                    batch=batch, tq=1024, cw=256)
    oc = _conv(u, halo0, cw, conv_b, conv_norm_g, batch=batch, tl=256, rc=32, rb=8)
    x2 = _out_proj(x1, oa, oc, wout, tm=512)
    y = ffn(x2, ffn2_norm_g, w2g, w2u, w2d, final_norm_g, tm=512, final_norm=True)
    return y.reshape(batch, SEQ, D_MODEL)
```

```python
import functools
import math

import jax
import jax.numpy as jnp
from jax import lax
from jax.experimental import pallas as pl
from jax.experimental.pallas import tpu as pltpu

F32 = jnp.float32
BF16 = jnp.bfloat16

D_MODEL = 2048
SEQ = 4096
N_META = 16
CHUNK = 64
ATTN_WIDTH = 1024
CONV_WIDTH = 1024
HEAD_DIM = 128
N_HEADS = ATTN_WIDTH // HEAD_DIM
QK_DIM = HEAD_DIM // 2
CONV_KERNEL = 31
D_FF = 5632
EPS = 1e-6
NEG_INF = -1e30
LAM_INIT = 0.8 - 0.6 * math.exp(-0.3 * 1)

LOG2E = math.log2(math.e)

LANES = 128
SUBLANES = 8
DENOM_ROWS = 16
HALO = 32
VMEM_LIMIT = 56 * 1024 * 1024

NT_DIMS = (((1,), (1,)), ((), ()))


def _params(*sem):
    return pltpu.CompilerParams(dimension_semantics=sem, vmem_limit_bytes=VMEM_LIMIT)


def _rms(x, g):
    ms = jnp.mean(x * x, axis=-1, keepdims=True)
    return x * lax.rsqrt(ms + EPS) * g


def _sigmoid(x):
    return 1.0 / (1.0 + jnp.exp(-x))


def _ffn_kernel(x_ref, g_ref, wg_ref, wu_ref, wd_ref, fg_ref, o_ref, h_ref, *, final_norm):
    j = pl.program_id(1)

    @pl.when(j == 0)
    def _():
        x = x_ref[...]
        h_ref[...] = _rms(x, g_ref[...]).astype(BF16)
        o_ref[...] = x

    h = h_ref[...]
    gate = jnp.dot(h, wg_ref[...].astype(BF16), preferred_element_type=F32)
    up = jnp.dot(h, wu_ref[...].astype(BF16), preferred_element_type=F32)
    a = (gate * _sigmoid(gate)) * (up * 0.5)
    o_ref[...] += jnp.dot(a.astype(BF16), wd_ref[...].astype(BF16), preferred_element_type=F32)

    if final_norm:
        @pl.when(j == pl.num_programs(1) - 1)
        def _():
            o_ref[...] = _rms(o_ref[...], fg_ref[...])


def _ffn(x, g, wg, wu, wd, fg, *, tm, tf, final_norm):
    m = x.shape[0]
    grid = (m // tm, D_FF // tf)
    return pl.pallas_call(
        functools.partial(_ffn_kernel, final_norm=final_norm),
        grid=grid,
        in_specs=[
            pl.BlockSpec((tm, D_MODEL), lambda i, j: (i, 0)),
            pl.BlockSpec((1, D_MODEL), lambda i, j: (0, 0)),
            pl.BlockSpec((D_MODEL, tf), lambda i, j: (0, j)),
            pl.BlockSpec((D_MODEL, tf), lambda i, j: (0, j)),
            pl.BlockSpec((tf, D_MODEL), lambda i, j: (j, 0)),
            pl.BlockSpec((1, D_MODEL), lambda i, j: (0, 0)),
        ],
        out_specs=pl.BlockSpec((tm, D_MODEL), lambda i, j: (i, 0)),
        out_shape=jax.ShapeDtypeStruct((m, D_MODEL), F32),
        scratch_shapes=[pltpu.VMEM((tm, D_MODEL), BF16)],
        compiler_params=_params("parallel", "arbitrary"),
        name="ffn_final" if final_norm else "ffn",
    )(x, g, wg, wu, wd, fg)


def _qk_norm(x, g):
    outs = []
    for hh in range(x.shape[1] // LANES):
        xh = x[:, hh * LANES:(hh + 1) * LANES]
        lo = lax.broadcasted_iota(jnp.int32, xh.shape, 1) < QK_DIM
        x2 = xh * xh
        s_lo = jnp.sum(jnp.where(lo, x2, 0.0), axis=-1, keepdims=True)
        s_hi = jnp.sum(jnp.where(lo, 0.0, x2), axis=-1, keepdims=True)
        ms = jnp.where(lo, s_lo, s_hi) * (1.0 / QK_DIM)
        outs.append(xh * lax.rsqrt(ms + EPS) * g)
    return outs[0] if len(outs) == 1 else jnp.concatenate(outs, axis=1)


def _inproj_kernel(x_ref, g_ref, wq_ref, wk_ref, wv_ref, wa_ref, wc_ref, qg_ref, kg_ref,
                   q_ref, k_ref, vt_ref, u_ref, h_ref):
    j = pl.program_id(1)

    @pl.when(j == 0)
    def _():
        h_ref[...] = _rms(x_ref[...], g_ref[...]).astype(BF16)

    h = h_ref[...]
    proj = lambda w_ref: jnp.dot(h, w_ref[...].astype(BF16), preferred_element_type=F32)
    q_ref[...] = _qk_norm(proj(wq_ref), qg_ref[...]).astype(BF16)
    k_ref[...] = _qk_norm(proj(wk_ref), kg_ref[...]).astype(BF16)
    vt_ref[...] = proj(wv_ref).T.astype(BF16)
    u_ref[...] = proj(wa_ref) * _sigmoid(proj(wc_ref))


def _in_proj(x, g, w_in, qg, kg, *, tm, tn):
    m = x.shape[0]
    nb = ATTN_WIDTH // tn
    wspec = lambda c: pl.BlockSpec((D_MODEL, tn), lambda i, j, c=c: (0, c * nb + j))
    return pl.pallas_call(
        _inproj_kernel,
        grid=(m // tm, nb),
        in_specs=[
            pl.BlockSpec((tm, D_MODEL), lambda i, j: (i, 0)),
            pl.BlockSpec((1, D_MODEL), lambda i, j: (0, 0)),
            wspec(0), wspec(1), wspec(2), wspec(3), wspec(4),
            pl.BlockSpec((1, LANES), lambda i, j: (0, 0)),
            pl.BlockSpec((1, LANES), lambda i, j: (0, 0)),
        ],
        out_specs=[
            pl.BlockSpec((tm, tn), lambda i, j: (i, j)),
            pl.BlockSpec((tm, tn), lambda i, j: (i, j)),
            pl.BlockSpec((tn, tm), lambda i, j: (j, i)),
            pl.BlockSpec((tm, tn), lambda i, j: (i, j)),
        ],
        out_shape=[
            jax.ShapeDtypeStruct((m, ATTN_WIDTH), BF16),
            jax.ShapeDtypeStruct((m, ATTN_WIDTH), BF16),
            jax.ShapeDtypeStruct((ATTN_WIDTH, m), BF16),
            jax.ShapeDtypeStruct((m, CONV_WIDTH), F32),
        ],
        scratch_shapes=[pltpu.VMEM((tm, D_MODEL), BF16)],
        compiler_params=_params("parallel", "arbitrary"),
        name="in_proj",
    )(x, g, w_in, w_in, w_in, w_in, w_in, qg, kg)


def _attn_kernel(q_ref, k_ref, vt_ref, km_ref, vmt_ref, lq1_ref, lk1_ref, lq2_ref, lk2_ref, sg_ref,
                 o_ref, m_sc, acc_sc, s0_sc, s1_sc, *, tq, tk, cw):
    i = pl.program_id(2)
    q = q_ref[...]
    lane = lax.broadcasted_iota(jnp.int32, q.shape, 1)
    zero = jnp.zeros_like(q)
    qs = jnp.concatenate([jnp.where(lane < QK_DIM, q, zero),
                          jnp.where(lane < QK_DIM, zero, q)], axis=0)

    def scores(blk, s_ref):
        off = pl.multiple_of(blk * tk, tk)
        s_ref[...] = lax.dot_general(k_ref[pl.ds(off, tk), :], qs, NT_DIMS, preferred_element_type=F32)

    def with_ones(vt):
        return jnp.concatenate([vt, jnp.ones((DENOM_ROWS, vt.shape[1]), BF16)], axis=0)

    def online_update(s, vt, m_old, acc_old):
        m_new = jnp.maximum(m_old, jnp.max(s, axis=0, keepdims=True))
        alpha = jnp.exp2(m_old - m_new)
        p = jnp.exp2(s - m_new)
        return m_new, alpha * acc_old + jnp.dot(with_ones(vt), p.astype(BF16), preferred_element_type=F32)

    def softmax_pv(blk, s_ref, diag):
        off = pl.multiple_of(blk * tk, tk)
        for c in range(2 * tq // cw):
            cs = slice(c * cw, (c + 1) * cw)
            nk, masked = tk, False
            if diag is not None:
                q0 = (c * cw) % tq
                nk = min(tk, q0 + cw - diag * tk)
                if nk <= 0:
                    continue
                masked = (diag * tk + nk - 1) // CHUNK > q0 // CHUNK
            s = s_ref[0:nk, cs]
            if masked:
                kchunk = (lax.broadcasted_iota(jnp.int32, (nk, cw), 0) + diag * tk) // CHUNK
                qchunk = (lax.broadcasted_iota(jnp.int32, (nk, cw), 1) + q0) // CHUNK
                s = jnp.where(kchunk <= qchunk, s, NEG_INF)
            m_sc[:, cs], acc_sc[:, cs] = online_update(s, vt_ref[:, pl.ds(off, nk)], m_sc[:, cs], acc_sc[:, cs])

    scores(0, s0_sc)
    m_sc[...] = jnp.full(m_sc.shape, NEG_INF, F32)
    acc_sc[...] = jnp.zeros(acc_sc.shape, F32)

    def pair(jp, carry):
        scores(2 * jp + 1, s1_sc)
        softmax_pv(2 * jp, s0_sc, None)
        scores(2 * jp + 2, s0_sc)
        softmax_pv(2 * jp + 1, s1_sc, None)
        return carry

    lax.fori_loop(0, i, pair, 0)

    scores(2 * i + 1, s1_sc)
    softmax_pv(2 * i, s0_sc, 0)
    softmax_pv(2 * i + 1, s1_sc, 1)

    lam = (jnp.exp(jnp.sum(lq1_ref[...] * lk1_ref[...], axis=-1, keepdims=True))
           - jnp.exp(jnp.sum(lq2_ref[...] * lk2_ref[...], axis=-1, keepdims=True)) + LAM_INIT)
    s = lax.dot_general(km_ref[...], qs, NT_DIMS, preferred_element_type=F32)
    _, acc = online_update(s, vmt_ref[...], m_sc[...], acc_sc[...])
    o = acc[:HEAD_DIM] * (1.0 / acc[HEAD_DIM:HEAD_DIM + 1])
    od = o[:, :tq] - lam * o[:, tq:]
    ms = jnp.mean(od * od, axis=0, keepdims=True)
    on = od * lax.rsqrt(ms + EPS) * sg_ref[...] * (1.0 - LAM_INIT)
    o_ref[...] = on.T.astype(BF16)


def _attention(q, k, vt, km, vmt, lq1, lk1, lq2, lk2, sg, *, batch, tq, cw):
    tk = tq // 2
    assert tk % cw == 0 and cw % CHUNK == 0
    nq = SEQ // tq
    lspec = pl.BlockSpec((1, QK_DIM), lambda b, h, i: (0, 0))
    return pl.pallas_call(
        functools.partial(_attn_kernel, tq=tq, tk=tk, cw=cw),
        grid=(batch, N_HEADS, nq),
        in_specs=[
            pl.BlockSpec((tq, HEAD_DIM), lambda b, h, i: (b * nq + i, h)),
            pl.BlockSpec((SEQ, HEAD_DIM), lambda b, h, i: (b, h)),
            pl.BlockSpec((HEAD_DIM, SEQ), lambda b, h, i: (h, b)),
            pl.BlockSpec((N_META, HEAD_DIM), lambda b, h, i: (0, h)),
            pl.BlockSpec((HEAD_DIM, N_META), lambda b, h, i: (h, 0)),
            lspec, lspec, lspec, lspec,
            pl.BlockSpec((HEAD_DIM, 1), lambda b, h, i: (0, 0)),
        ],
        out_specs=pl.BlockSpec((tq, HEAD_DIM), lambda b, h, i: (b * nq + i, h)),
        out_shape=jax.ShapeDtypeStruct((batch * SEQ, ATTN_WIDTH), BF16),
        scratch_shapes=[
            pltpu.VMEM((1, 2 * tq), F32),
            pltpu.VMEM((HEAD_DIM + DENOM_ROWS, 2 * tq), F32),
            pltpu.VMEM((tk, 2 * tq), F32),
            pltpu.VMEM((tk, 2 * tq), F32),
        ],
        compiler_params=_params("parallel", "parallel", "arbitrary"),
        name="diff_attn",
    )(q, k, vt, km, vmt, lq1, lk1, lq2, lk2, sg)


def _conv_kernel(u_ref, prev_ref, halo0_ref, w_ref, b_ref, g_ref, o_ref, win_sc, shift_sc, y_sc, *, tl, rc, rb):
    i = pl.program_id(1)

    @pl.when(i == 0)
    def _():
        win_sc[0:HALO, :] = halo0_ref[...]

    @pl.when(i > 0)
    def _():
        win_sc[0:HALO, :] = prev_ref[...]

    win_sc[HALO:HALO + tl, :] = u_ref[...]

    nsh = HALO + tl - SUBLANES
    for s in range(1, SUBLANES):
        shift_sc[s - 1] = win_sc[s:s + nsh, :]

    first = HALO - (CONV_KERNEL - 1)
    @pl.loop(0, CONV_WIDTH // LANES)
    def _(c):
        cs = pl.ds(pl.multiple_of(c * LANES, LANES), LANES)
        for r0 in range(0, tl // rc, rb):
            accs = [jnp.broadcast_to(b_ref[:, cs], (rc, LANES))] * rb
            for t in range(CONV_KERNEL):
                wt = w_ref[t:t + 1, cs]
                for k in range(rb):
                    lo = (r0 + k) * rc + first + t
                    s, al = lo % SUBLANES, lo - lo % SUBLANES
                    tap = win_sc[al:al + rc, cs] if s == 0 else shift_sc[s - 1, al:al + rc, cs]
                    accs[k] = accs[k] + wt * tap
            for k in range(rb):
                y_sc[(r0 + k) * rc:(r0 + k + 1) * rc, cs] = accs[k]

    yn = _rms(y_sc[...], g_ref[...])
    o_ref[...] = (yn * _sigmoid(yn)).astype(BF16)


def _conv(u, halo0, w, b, g, *, batch, tl, rc, rb):
    nt = SEQ // tl
    hb = tl // HALO
    return pl.pallas_call(
        functools.partial(_conv_kernel, tl=tl, rc=rc, rb=rb),
        grid=(batch, nt),
        in_specs=[
            pl.BlockSpec((tl, CONV_WIDTH), lambda bb, i: (bb * nt + i, 0)),
            pl.BlockSpec((HALO, CONV_WIDTH), lambda bb, i: (jnp.maximum((bb * nt + i) * hb - 1, 0), 0)),
            pl.BlockSpec((HALO, CONV_WIDTH), lambda bb, i: (0, 0)),
            pl.BlockSpec((CONV_KERNEL, CONV_WIDTH), lambda bb, i: (0, 0)),
            pl.BlockSpec((1, CONV_WIDTH), lambda bb, i: (0, 0)),
            pl.BlockSpec((1, CONV_WIDTH), lambda bb, i: (0, 0)),
        ],
        out_specs=pl.BlockSpec((tl, CONV_WIDTH), lambda bb, i: (bb * nt + i, 0)),
        out_shape=jax.ShapeDtypeStruct((batch * SEQ, CONV_WIDTH), BF16),
        scratch_shapes=[
            pltpu.VMEM((HALO + tl, CONV_WIDTH), F32),
            pltpu.VMEM((SUBLANES - 1, HALO + tl - SUBLANES, CONV_WIDTH), F32),
            pltpu.VMEM((tl, CONV_WIDTH), F32),
        ],
        compiler_params=_params("parallel", "arbitrary"),
        name="conv",
    )(u, u, halo0, w, b, g)


def _outproj_kernel(x_ref, a_ref, c_ref, wa_ref, wc_ref, o_ref):
    o_ref[...] = (x_ref[...]
                  + jnp.dot(a_ref[...], wa_ref[...].astype(BF16), preferred_element_type=F32)
                  + jnp.dot(c_ref[...], wc_ref[...].astype(BF16), preferred_element_type=F32))


def _out_proj(x, oa, oc, w_out, *, tm, tn):
    m = x.shape[0]
    return pl.pallas_call(
        _outproj_kernel,
        grid=(m // tm, D_MODEL // tn),
        in_specs=[
            pl.BlockSpec((tm, tn), lambda i, j: (i, j)),
            pl.BlockSpec((tm, ATTN_WIDTH), lambda i, j: (i, 0)),
            pl.BlockSpec((tm, CONV_WIDTH), lambda i, j: (i, 0)),
            pl.BlockSpec((ATTN_WIDTH, tn), lambda i, j: (0, j)),
            pl.BlockSpec((CONV_WIDTH, tn), lambda i, j: (1, j)),
        ],
        out_specs=pl.BlockSpec((tm, tn), lambda i, j: (i, j)),
        out_shape=jax.ShapeDtypeStruct((m, D_MODEL), F32),
        compiler_params=_params("parallel", "arbitrary"),
        name="out_proj",
    )(x, oa, oc, w_out, w_out)


def kernel(x, meta_tokens, ffn1_norm_g, ffn1_w_gate, ffn1_w_up, ffn1_w_down, mix_norm_g, w_in, q_norm_g,
           k_norm_g, lambda_q1, lambda_k1, lambda_q2, lambda_k2, attn_subln_g, conv_w, conv_b, conv_norm_g,
           w_out, ffn2_norm_g, ffn2_w_gate, ffn2_w_up, ffn2_w_down, final_norm_g):
    batch = x.shape[0]
    xr = x.reshape(batch * SEQ, D_MODEL)

    w1g, w1u, w1d = ffn1_w_gate[0], ffn1_w_up[0], ffn1_w_down[0]
    w2g, w2u, w2d = ffn2_w_gate[0], ffn2_w_up[0], ffn2_w_down[0]
    win, wout = w_in[0], w_out[0]
    qg = jnp.tile(q_norm_g[0] * (QK_DIM ** -0.5 * LOG2E), 2)[None, :]
    kg = jnp.tile(k_norm_g[0], 2)[None, :]
    sg = attn_subln_g[0][:, None]
    cw = conv_w[0].reshape(CONV_KERNEL, CONV_WIDTH)

    ffn = functools.partial(_ffn, tm=1024, tf=256)
    m1 = _ffn(meta_tokens, ffn1_norm_g, w1g, w1u, w1d, final_norm_g, tm=N_META, tf=512, final_norm=False)
    _, km, vmt, um = _in_proj(m1, mix_norm_g, win, qg, kg, tm=N_META, tn=512)
    halo0 = jnp.concatenate([jnp.zeros((HALO - N_META, CONV_WIDTH), F32), um], axis=0)

    x1 = ffn(xr, ffn1_norm_g, w1g, w1u, w1d, final_norm_g, final_norm=False)
    q, k, vt, u = _in_proj(x1, mix_norm_g, win, qg, kg, tm=1024, tn=256)
    oa = _attention(q, k, vt, km, vmt, lambda_q1, lambda_k1, lambda_q2, lambda_k2, sg,
                    batch=batch, tq=1024, cw=256)
    oc = _conv(u, halo0, cw, conv_b, conv_norm_g, batch=batch, tl=256, rc=32, rb=8)
    x2 = _out_proj(x1, oa, oc, wout, tm=1024, tn=512)
    y = ffn(x2, ffn2_norm_g, w2g, w2u, w2d, final_norm_g, final_norm=True)
    return y.reshape(batch, SEQ, D_MODEL)
```

```python
import functools
import math

import jax
import jax.numpy as jnp
from jax import lax
from jax.experimental import pallas as pl
from jax.experimental.pallas import tpu as pltpu

F32 = jnp.float32
BF16 = jnp.bfloat16

D_MODEL = 2048
SEQ = 4096
N_META = 16
CHUNK = 64
ATTN_WIDTH = 1024
CONV_WIDTH = 1024
HEAD_DIM = 128
N_HEADS = ATTN_WIDTH // HEAD_DIM
QK_DIM = HEAD_DIM // 2
CONV_KERNEL = 31
D_FF = 5632
EPS = 1e-6
NEG_INF = -1e30
LAM_INIT = 0.8 - 0.6 * math.exp(-0.3 * 1)

LOG2E = math.log2(math.e)

LANES = 128
SUBLANES = 8
DENOM_ROWS = 16
HALO = 32
VMEM_LIMIT = 56 * 1024 * 1024

NT_DIMS = (((1,), (1,)), ((), ()))


def _params(*sem):
    return pltpu.CompilerParams(dimension_semantics=sem, vmem_limit_bytes=VMEM_LIMIT)


def _rms(x, g):
    ms = jnp.mean(x * x, axis=-1, keepdims=True)
    return x * lax.rsqrt(ms + EPS) * g


def _sigmoid(x):
    return 1.0 / (1.0 + jnp.exp(-x))


def _ffn_kernel(x_ref, g_ref, wg_ref, wu_ref, wd_ref, fg_ref, o_ref, h_ref, *, final_norm):
    j = pl.program_id(1)

    @pl.when(j == 0)
    def _():
        x = x_ref[...]
        h_ref[...] = _rms(x, g_ref[...]).astype(BF16)
        o_ref[...] = x

    h = h_ref[...]
    gate = jnp.dot(h, wg_ref[...].astype(BF16), preferred_element_type=F32)
    up = jnp.dot(h, wu_ref[...].astype(BF16), preferred_element_type=F32)
    a = (gate * _sigmoid(gate)) * (up * 0.5)
    o_ref[...] += jnp.dot(a.astype(BF16), wd_ref[...].astype(BF16), preferred_element_type=F32)

    if final_norm:
        @pl.when(j == pl.num_programs(1) - 1)
        def _():
            o_ref[...] = _rms(o_ref[...], fg_ref[...])


def _ffn(x, g, wg, wu, wd, fg, *, tm, tf, final_norm):
    m = x.shape[0]
    grid = (m // tm, D_FF // tf)
    return pl.pallas_call(
        functools.partial(_ffn_kernel, final_norm=final_norm),
        grid=grid,
        in_specs=[
            pl.BlockSpec((tm, D_MODEL), lambda i, j: (i, 0)),
            pl.BlockSpec((1, D_MODEL), lambda i, j: (0, 0)),
            pl.BlockSpec((D_MODEL, tf), lambda i, j: (0, j)),
            pl.BlockSpec((D_MODEL, tf), lambda i, j: (0, j)),
            pl.BlockSpec((tf, D_MODEL), lambda i, j: (j, 0)),
            pl.BlockSpec((1, D_MODEL), lambda i, j: (0, 0)),
        ],
        out_specs=pl.BlockSpec((tm, D_MODEL), lambda i, j: (i, 0)),
        out_shape=jax.ShapeDtypeStruct((m, D_MODEL), F32),
        scratch_shapes=[pltpu.VMEM((tm, D_MODEL), BF16)],
        compiler_params=_params("parallel", "arbitrary"),
        name="ffn_final" if final_norm else "ffn",
    )(x, g, wg, wu, wd, fg)


def _qk_norm(x, g):
    outs = []
    for hh in range(x.shape[1] // LANES):
        xh = x[:, hh * LANES:(hh + 1) * LANES]
        lo = lax.broadcasted_iota(jnp.int32, xh.shape, 1) < QK_DIM
        x2 = xh * xh
        s_lo = jnp.sum(jnp.where(lo, x2, 0.0), axis=-1, keepdims=True)
        s_hi = jnp.sum(jnp.where(lo, 0.0, x2), axis=-1, keepdims=True)
        ms = jnp.where(lo, s_lo, s_hi) * (1.0 / QK_DIM)
        outs.append(xh * lax.rsqrt(ms + EPS) * g)
    return outs[0] if len(outs) == 1 else jnp.concatenate(outs, axis=1)


def _inproj_kernel(x_ref, g_ref, wq_ref, wk_ref, wv_ref, wa_ref, wc_ref, qg_ref, kg_ref,
                   q_ref, k_ref, vt_ref, u_ref, h_ref):
    j = pl.program_id(1)

    @pl.when(j == 0)
    def _():
        h_ref[...] = _rms(x_ref[...], g_ref[...]).astype(BF16)

    h = h_ref[...]
    proj = lambda w_ref: jnp.dot(h, w_ref[...].astype(BF16), preferred_element_type=F32)
    q_ref[...] = _qk_norm(proj(wq_ref), qg_ref[...]).astype(BF16)
    k_ref[...] = _qk_norm(proj(wk_ref), kg_ref[...]).astype(BF16)
    vt_ref[...] = proj(wv_ref).T.astype(BF16)
    u_ref[...] = proj(wa_ref) * _sigmoid(proj(wc_ref))


def _in_proj(x, g, w_in, qg, kg, *, tm, tn):
    m = x.shape[0]
    nb = ATTN_WIDTH // tn
    wspec = lambda c: pl.BlockSpec((D_MODEL, tn), lambda i, j, c=c: (0, c * nb + j))
    return pl.pallas_call(
        _inproj_kernel,
        grid=(m // tm, nb),
        in_specs=[
            pl.BlockSpec((tm, D_MODEL), lambda i, j: (i, 0)),
            pl.BlockSpec((1, D_MODEL), lambda i, j: (0, 0)),
            wspec(0), wspec(1), wspec(2), wspec(3), wspec(4),
            pl.BlockSpec((1, LANES), lambda i, j: (0, 0)),
            pl.BlockSpec((1, LANES), lambda i, j: (0, 0)),
        ],
        out_specs=[
            pl.BlockSpec((tm, tn), lambda i, j: (i, j)),
            pl.BlockSpec((tm, tn), lambda i, j: (i, j)),
            pl.BlockSpec((tn, tm), lambda i, j: (j, i)),
            pl.BlockSpec((tm, tn), lambda i, j: (i, j)),
        ],
        out_shape=[
            jax.ShapeDtypeStruct((m, ATTN_WIDTH), BF16),
            jax.ShapeDtypeStruct((m, ATTN_WIDTH), BF16),
            jax.ShapeDtypeStruct((ATTN_WIDTH, m), BF16),
            jax.ShapeDtypeStruct((m, CONV_WIDTH), F32),
        ],
        scratch_shapes=[pltpu.VMEM((tm, D_MODEL), BF16)],
        compiler_params=_params("parallel", "arbitrary"),
        name="in_proj",
    )(x, g, w_in, w_in, w_in, w_in, w_in, qg, kg)


def _attn_kernel(q_ref, k_ref, vt_ref, km_ref, vmt_ref, lq1_ref, lk1_ref, lq2_ref, lk2_ref, sg_ref,
                 o_ref, m_sc, acc_sc, s0_sc, s1_sc, *, tq, tk, cw):
    i = pl.program_id(2)
    q = q_ref[...]
    lane = lax.broadcasted_iota(jnp.int32, q.shape, 1)
    zero = jnp.zeros_like(q)
    qs = jnp.concatenate([jnp.where(lane < QK_DIM, q, zero),
                          jnp.where(lane < QK_DIM, zero, q)], axis=0)

    def scores(blk, s_ref):
        s_ref[...] = lax.dot_general(k_ref[blk * tk:(blk + 1) * tk, :], qs, NT_DIMS,
                                     preferred_element_type=F32)

    def with_ones(vt):
        return jnp.concatenate([vt, jnp.ones((DENOM_ROWS, vt.shape[1]), BF16)], axis=0)

    def online_update(s, vt, m_old, acc_old):
        m_new = jnp.maximum(m_old, jnp.max(s, axis=0, keepdims=True))
        alpha = jnp.exp2(m_old - m_new)
        p = jnp.exp2(s - m_new)
        return m_new, alpha * acc_old + jnp.dot(with_ones(vt), p.astype(BF16), preferred_element_type=F32)

    def softmax_pv(blk, s_ref, diag):
        off = blk * tk
        for c in range(2 * tq // cw):
            cs = slice(c * cw, (c + 1) * cw)
            nk, masked = tk, False
            if diag is not None:
                q0 = (c * cw) % tq
                nk = min(tk, q0 + cw - diag * tk)
                if nk <= 0:
                    continue
                masked = (diag * tk + nk - 1) // CHUNK > q0 // CHUNK
            s = s_ref[0:nk, cs]
            if masked:
                kchunk = (lax.broadcasted_iota(jnp.int32, (nk, cw), 0) + diag * tk) // CHUNK
                qchunk = (lax.broadcasted_iota(jnp.int32, (nk, cw), 1) + q0) // CHUNK
                s = jnp.where(kchunk <= qchunk, s, NEG_INF)
            m_sc[:, cs], acc_sc[:, cs] = online_update(s, vt_ref[:, off:off + nk], m_sc[:, cs], acc_sc[:, cs])

    def run(n_pairs):
        scores(0, s0_sc)
        s = lax.dot_general(km_ref[...], qs, NT_DIMS, preferred_element_type=F32)
        m_sc[...], acc_sc[...] = online_update(s, vmt_ref[...], jnp.full(m_sc.shape, NEG_INF, F32),
                                               jnp.zeros(acc_sc.shape, F32))
        for jp in range(n_pairs):
            scores(2 * jp + 1, s1_sc)
            softmax_pv(2 * jp, s0_sc, None)
            scores(2 * jp + 2, s0_sc)
            softmax_pv(2 * jp + 1, s1_sc, None)

        scores(2 * n_pairs + 1, s1_sc)
        softmax_pv(2 * n_pairs, s0_sc, 0)
        softmax_pv(2 * n_pairs + 1, s1_sc, 1)

        lam = (jnp.exp(jnp.sum(lq1_ref[...] * lk1_ref[...], axis=-1, keepdims=True))
               - jnp.exp(jnp.sum(lq2_ref[...] * lk2_ref[...], axis=-1, keepdims=True)) + LAM_INIT)
        acc = acc_sc[...]
        o = acc[:HEAD_DIM] * (1.0 / acc[HEAD_DIM:HEAD_DIM + 1])
        od = o[:, :tq] - lam * o[:, tq:]
        ms = jnp.mean(od * od, axis=0, keepdims=True)
        on = od * lax.rsqrt(ms + EPS) * sg_ref[...] * (1.0 - LAM_INIT)
        o_ref[...] = on.T.astype(BF16)

    for n in range(SEQ // tq):
        pl.when(i == n)(functools.partial(run, n))


def _attention(q, k, vt, km, vmt, lq1, lk1, lq2, lk2, sg, *, batch, tq, cw):
    tk = tq // 2
    assert tk % cw == 0 and cw % CHUNK == 0
    nq = SEQ // tq
    lspec = pl.BlockSpec((1, QK_DIM), lambda b, h, i: (0, 0))
    return pl.pallas_call(
        functools.partial(_attn_kernel, tq=tq, tk=tk, cw=cw),
        grid=(batch, N_HEADS, nq),
        in_specs=[
            pl.BlockSpec((tq, HEAD_DIM), lambda b, h, i: (b * nq + i, h)),
            pl.BlockSpec((SEQ, HEAD_DIM), lambda b, h, i: (b, h)),
            pl.BlockSpec((HEAD_DIM, SEQ), lambda b, h, i: (h, b)),
            pl.BlockSpec((N_META, HEAD_DIM), lambda b, h, i: (0, h)),
            pl.BlockSpec((HEAD_DIM, N_META), lambda b, h, i: (h, 0)),
            lspec, lspec, lspec, lspec,
            pl.BlockSpec((HEAD_DIM, 1), lambda b, h, i: (0, 0)),
        ],
        out_specs=pl.BlockSpec((tq, HEAD_DIM), lambda b, h, i: (b * nq + i, h)),
        out_shape=jax.ShapeDtypeStruct((batch * SEQ, ATTN_WIDTH), BF16),
        scratch_shapes=[
            pltpu.VMEM((1, 2 * tq), F32),
            pltpu.VMEM((HEAD_DIM + DENOM_ROWS, 2 * tq), F32),
            pltpu.VMEM((tk, 2 * tq), F32),
            pltpu.VMEM((tk, 2 * tq), F32),
        ],
        compiler_params=_params("parallel", "parallel", "arbitrary"),
        name="diff_attn",
    )(q, k, vt, km, vmt, lq1, lk1, lq2, lk2, sg)


def _conv_kernel(u_ref, prev_ref, halo0_ref, w_ref, b_ref, g_ref, o_ref, win_sc, shift_sc, y_sc, *, tl, rc, rb):
    i = pl.program_id(1)

    @pl.when(i == 0)
    def _():
        win_sc[0:HALO, :] = halo0_ref[...]

    @pl.when(i > 0)
    def _():
        win_sc[0:HALO, :] = prev_ref[...]

    win_sc[HALO:HALO + tl, :] = u_ref[...]

    nsh = HALO + tl - SUBLANES
    for s in range(1, SUBLANES):
        shift_sc[s - 1] = win_sc[s:s + nsh, :]

    first = HALO - (CONV_KERNEL - 1)
    @pl.loop(0, CONV_WIDTH // LANES)
    def _(c):
        cs = pl.ds(pl.multiple_of(c * LANES, LANES), LANES)
        for r0 in range(0, tl // rc, rb):
            accs = [jnp.broadcast_to(b_ref[:, cs], (rc, LANES))] * rb
            for t in range(CONV_KERNEL):
                wt = w_ref[t:t + 1, cs]
                for k in range(rb):
                    lo = (r0 + k) * rc + first + t
                    s, al = lo % SUBLANES, lo - lo % SUBLANES
                    tap = win_sc[al:al + rc, cs] if s == 0 else shift_sc[s - 1, al:al + rc, cs]
                    accs[k] = accs[k] + wt * tap
            for k in range(rb):
                y_sc[(r0 + k) * rc:(r0 + k + 1) * rc, cs] = accs[k]

    yn = _rms(y_sc[...], g_ref[...])
    o_ref[...] = (yn * _sigmoid(yn)).astype(BF16)


def _conv(u, halo0, w, b, g, *, batch, tl, rc, rb):
    nt = SEQ // tl
    hb = tl // HALO
    return pl.pallas_call(
        functools.partial(_conv_kernel, tl=tl, rc=rc, rb=rb),
        grid=(batch, nt),
        in_specs=[
            pl.BlockSpec((tl, CONV_WIDTH), lambda bb, i: (bb * nt + i, 0)),
            pl.BlockSpec((HALO, CONV_WIDTH), lambda bb, i: (jnp.maximum((bb * nt + i) * hb - 1, 0), 0)),
            pl.BlockSpec((HALO, CONV_WIDTH), lambda bb, i: (0, 0)),
            pl.BlockSpec((CONV_KERNEL, CONV_WIDTH), lambda bb, i: (0, 0)),
            pl.BlockSpec((1, CONV_WIDTH), lambda bb, i: (0, 0)),
            pl.BlockSpec((1, CONV_WIDTH), lambda bb, i: (0, 0)),
        ],
        out_specs=pl.BlockSpec((tl, CONV_WIDTH), lambda bb, i: (bb * nt + i, 0)),
        out_shape=jax.ShapeDtypeStruct((batch * SEQ, CONV_WIDTH), BF16),
        scratch_shapes=[
            pltpu.VMEM((HALO + tl, CONV_WIDTH), F32),
            pltpu.VMEM((SUBLANES - 1, HALO + tl - SUBLANES, CONV_WIDTH), F32),
            pltpu.VMEM((tl, CONV_WIDTH), F32),
        ],
        compiler_params=_params("parallel", "arbitrary"),
        name="conv",
    )(u, u, halo0, w, b, g)


def _outproj_kernel(x_ref, a_ref, c_ref, wa_ref, wc_ref, o_ref):
    o_ref[...] = (x_ref[...]
                  + jnp.dot(a_ref[...], wa_ref[...].astype(BF16), preferred_element_type=F32)
                  + jnp.dot(c_ref[...], wc_ref[...].astype(BF16), preferred_element_type=F32))


def _out_proj(x, oa, oc, w_out, *, tm, tn):
    m = x.shape[0]
    return pl.pallas_call(
        _outproj_kernel,
        grid=(m // tm, D_MODEL // tn),
        in_specs=[
            pl.BlockSpec((tm, tn), lambda i, j: (i, j)),
            pl.BlockSpec((tm, ATTN_WIDTH), lambda i, j: (i, 0)),
            pl.BlockSpec((tm, CONV_WIDTH), lambda i, j: (i, 0)),
            pl.BlockSpec((ATTN_WIDTH, tn), lambda i, j: (0, j)),
            pl.BlockSpec((CONV_WIDTH, tn), lambda i, j: (1, j)),
        ],
        out_specs=pl.BlockSpec((tm, tn), lambda i, j: (i, j)),
        out_shape=jax.ShapeDtypeStruct((m, D_MODEL), F32),
        compiler_params=_params("parallel", "arbitrary"),
        name="out_proj",
    )(x, oa, oc, w_out, w_out)


def kernel(x, meta_tokens, ffn1_norm_g, ffn1_w_gate, ffn1_w_up, ffn1_w_down, mix_norm_g, w_in, q_norm_g,
           k_norm_g, lambda_q1, lambda_k1, lambda_q2, lambda_k2, attn_subln_g, conv_w, conv_b, conv_norm_g,
           w_out, ffn2_norm_g, ffn2_w_gate, ffn2_w_up, ffn2_w_down, final_norm_g):
    batch = x.shape[0]
    xr = x.reshape(batch * SEQ, D_MODEL)

    w1g, w1u, w1d = ffn1_w_gate[0], ffn1_w_up[0], ffn1_w_down[0]
    w2g, w2u, w2d = ffn2_w_gate[0], ffn2_w_up[0], ffn2_w_down[0]
    win, wout = w_in[0], w_out[0]
    qg = jnp.tile(q_norm_g[0] * (QK_DIM ** -0.5 * LOG2E), 2)[None, :]
    kg = jnp.tile(k_norm_g[0], 2)[None, :]
    sg = attn_subln_g[0][:, None]
    cw = conv_w[0].reshape(CONV_KERNEL, CONV_WIDTH)

    ffn = functools.partial(_ffn, tm=1024, tf=256)
    m1 = _ffn(meta_tokens, ffn1_norm_g, w1g, w1u, w1d, final_norm_g, tm=N_META, tf=512, final_norm=False)
    _, km, vmt, um = _in_proj(m1, mix_norm_g, win, qg, kg, tm=N_META, tn=512)
    halo0 = jnp.concatenate([jnp.zeros((HALO - N_META, CONV_WIDTH), F32), um], axis=0)

    x1 = ffn(xr, ffn1_norm_g, w1g, w1u, w1d, final_norm_g, final_norm=False)
    q, k, vt, u = _in_proj(x1, mix_norm_g, win, qg, kg, tm=1024, tn=256)
    oa = _attention(q, k, vt, km, vmt, lambda_q1, lambda_k1, lambda_q2, lambda_k2, sg,
                    batch=batch, tq=1024, cw=256)
    oc = _conv(u, halo0, cw, conv_b, conv_norm_g, batch=batch, tl=256, rc=32, rb=8)
    x2 = _out_proj(x1, oa, oc, wout, tm=2048, tn=512)
    y = ffn(x2, ffn2_norm_g, w2g, w2u, w2d, final_norm_g, final_norm=True)
    return y.reshape(batch, SEQ, D_MODEL)
```

```python
import functools
import math

import jax
import jax.numpy as jnp
from jax import lax
from jax.experimental import pallas as pl
from jax.experimental.pallas import tpu as pltpu

F32 = jnp.float32
BF16 = jnp.bfloat16

D_MODEL = 2048
SEQ = 4096
N_META = 16
CHUNK = 64
ATTN_WIDTH = 1024
CONV_WIDTH = 1024
HEAD_DIM = 128
N_HEADS = ATTN_WIDTH // HEAD_DIM
QK_DIM = HEAD_DIM // 2
CONV_KERNEL = 31
D_FF = 5632
EPS = 1e-6
NEG_INF = -1e30
LAM_INIT = 0.8 - 0.6 * math.exp(-0.3 * 1)

LOG2E = math.log2(math.e)

LANES = 128
SUBLANES = 8
BF16_ROWS = 16
DENOM_ROWS = 16
HALO = 32
VMEM_LIMIT = 56 * 1024 * 1024

NT_DIMS = (((1,), (1,)), ((), ()))


def _params(*sem):
    return pltpu.CompilerParams(dimension_semantics=sem, vmem_limit_bytes=VMEM_LIMIT)


def _rms(x, g):
    ms = jnp.mean(x * x, axis=-1, keepdims=True)
    return x * lax.rsqrt(ms + EPS) * g


def _sigmoid(x):
    return 1.0 / (1.0 + jnp.exp(-x))


def _ffn_kernel(x_ref, g_ref, wg_ref, wu_ref, wd_ref, fg_ref, o_ref, h_ref, *, final_norm):
    j = pl.program_id(1)

    @pl.when(j == 0)
    def _():
        x = x_ref[...]
        h_ref[...] = _rms(x, g_ref[...]).astype(BF16)
        o_ref[...] = x

    h = h_ref[...]
    gate = jnp.dot(h, wg_ref[...].astype(BF16), preferred_element_type=F32)
    up = jnp.dot(h, wu_ref[...].astype(BF16), preferred_element_type=F32)
    a = (gate * _sigmoid(gate)) * (up * 0.5)
    o_ref[...] += jnp.dot(a.astype(BF16), wd_ref[...].astype(BF16), preferred_element_type=F32)

    if final_norm:
        @pl.when(j == pl.num_programs(1) - 1)
        def _():
            o_ref[...] = _rms(o_ref[...], fg_ref[...])


def _ffn(x, g, wg, wu, wd, fg, *, tm, tf, final_norm):
    m = x.shape[0]
    grid = (m // tm, D_FF // tf)
    return pl.pallas_call(
        functools.partial(_ffn_kernel, final_norm=final_norm),
        grid=grid,
        in_specs=[
            pl.BlockSpec((tm, D_MODEL), lambda i, j: (i, 0)),
            pl.BlockSpec((1, D_MODEL), lambda i, j: (0, 0)),
            pl.BlockSpec((D_MODEL, tf), lambda i, j: (0, j)),
            pl.BlockSpec((D_MODEL, tf), lambda i, j: (0, j)),
            pl.BlockSpec((tf, D_MODEL), lambda i, j: (j, 0)),
            pl.BlockSpec((1, D_MODEL), lambda i, j: (0, 0)),
        ],
        out_specs=pl.BlockSpec((tm, D_MODEL), lambda i, j: (i, 0)),
        out_shape=jax.ShapeDtypeStruct((m, D_MODEL), F32),
        scratch_shapes=[pltpu.VMEM((tm, D_MODEL), BF16)],
        compiler_params=_params("parallel", "arbitrary"),
        name="ffn_final" if final_norm else "ffn",
    )(x, g, wg, wu, wd, fg)


def _qk_norm(x, g):
    outs = []
    for hh in range(x.shape[1] // LANES):
        xh = x[:, hh * LANES:(hh + 1) * LANES]
        lo = lax.broadcasted_iota(jnp.int32, xh.shape, 1) < QK_DIM
        x2 = xh * xh
        s_lo = jnp.sum(jnp.where(lo, x2, 0.0), axis=-1, keepdims=True)
        s_hi = jnp.sum(jnp.where(lo, 0.0, x2), axis=-1, keepdims=True)
        ms = jnp.where(lo, s_lo, s_hi) * (1.0 / QK_DIM)
        outs.append(xh * lax.rsqrt(ms + EPS) * g)
    return outs[0] if len(outs) == 1 else jnp.concatenate(outs, axis=1)


def _inproj_kernel(x_ref, g_ref, wq_ref, wk_ref, wv_ref, wa_ref, wc_ref, qg_ref, kg_ref,
                   q_ref, k_ref, vt_ref, u_ref, h_ref):
    j = pl.program_id(1)

    @pl.when(j == 0)
    def _():
        h_ref[...] = _rms(x_ref[...], g_ref[...]).astype(BF16)

    h = h_ref[...]
    proj = lambda w_ref: jnp.dot(h, w_ref[...].astype(BF16), preferred_element_type=F32)
    q_ref[...] = _qk_norm(proj(wq_ref), qg_ref[...]).astype(BF16)
    k_ref[...] = _qk_norm(proj(wk_ref), kg_ref[...]).astype(BF16)
    vt_ref[...] = proj(wv_ref).T.astype(BF16)
    u_ref[...] = proj(wa_ref) * _sigmoid(proj(wc_ref))


def _in_proj(x, g, w_in, qg, kg, *, tm, tn):
    m = x.shape[0]
    nb = ATTN_WIDTH // tn
    wspec = lambda c: pl.BlockSpec((D_MODEL, tn), lambda i, j, c=c: (0, c * nb + j))
    return pl.pallas_call(
        _inproj_kernel,
        grid=(m // tm, nb),
        in_specs=[
            pl.BlockSpec((tm, D_MODEL), lambda i, j: (i, 0)),
            pl.BlockSpec((1, D_MODEL), lambda i, j: (0, 0)),
            wspec(0), wspec(1), wspec(2), wspec(3), wspec(4),
            pl.BlockSpec((1, LANES), lambda i, j: (0, 0)),
            pl.BlockSpec((1, LANES), lambda i, j: (0, 0)),
        ],
        out_specs=[
            pl.BlockSpec((tm, tn), lambda i, j: (i, j)),
            pl.BlockSpec((tm, tn), lambda i, j: (i, j)),
            pl.BlockSpec((tn, tm), lambda i, j: (j, i)),
            pl.BlockSpec((tm, tn), lambda i, j: (i, j)),
        ],
        out_shape=[
            jax.ShapeDtypeStruct((m, ATTN_WIDTH), BF16),
            jax.ShapeDtypeStruct((m, ATTN_WIDTH), BF16),
            jax.ShapeDtypeStruct((ATTN_WIDTH, m), BF16),
            jax.ShapeDtypeStruct((m, CONV_WIDTH), F32),
        ],
        scratch_shapes=[pltpu.VMEM((tm, D_MODEL), BF16)],
        compiler_params=_params("parallel", "arbitrary"),
        name="in_proj",
    )(x, g, w_in, w_in, w_in, w_in, w_in, qg, kg)


def _attn_kernel(q_ref, k_ref, vt_ref, km_ref, vmt_ref, lq1_ref, lk1_ref, lq2_ref, lk2_ref, sg_ref,
                 *rest, tq, tk, cw, n_cast):
    w32_refs, o_ref, w16_refs = rest[:n_cast], rest[n_cast], rest[n_cast + 1:2 * n_cast + 1]
    m_sc, acc_sc, s0_sc, s1_sc = rest[2 * n_cast + 1:]
    i = pl.program_id(2)
    q = q_ref[...]
    lane = lax.broadcasted_iota(jnp.int32, q.shape, 1)
    zero = jnp.zeros_like(q)
    qs = jnp.concatenate([jnp.where(lane < QK_DIM, q, zero),
                          jnp.where(lane < QK_DIM, zero, q)], axis=0)

    def scores(blk, s_ref):
        s_ref[...] = lax.dot_general(k_ref[blk * tk:(blk + 1) * tk, :], qs, NT_DIMS,
                                     preferred_element_type=F32)

    def with_ones(vt):
        return jnp.concatenate([vt, jnp.ones((DENOM_ROWS, vt.shape[1]), BF16)], axis=0)

    def online_update(s, vt, m_old, acc_old):
        m_new = jnp.maximum(m_old, jnp.max(s, axis=0, keepdims=True))
        alpha = jnp.exp2(m_old - m_new)
        p = jnp.exp2(s - m_new)
        return m_new, alpha * acc_old + jnp.dot(with_ones(vt), p.astype(BF16), preferred_element_type=F32)

    def softmax_pv(blk, s_ref, diag):
        off = blk * tk
        for c in range(2 * tq // cw):
            cs = slice(c * cw, (c + 1) * cw)
            nk, masked = tk, False
            if diag is not None:
                q0 = (c * cw) % tq
                nk = min(tk, q0 + cw - diag * tk)
                if nk <= 0:
                    continue
                masked = (diag * tk + nk - 1) // CHUNK > q0 // CHUNK
            s = s_ref[0:nk, cs]
            if masked:
                kchunk = (lax.broadcasted_iota(jnp.int32, (nk, cw), 0) + diag * tk) // CHUNK
                qchunk = (lax.broadcasted_iota(jnp.int32, (nk, cw), 1) + q0) // CHUNK
                s = jnp.where(kchunk <= qchunk, s, NEG_INF)
            m_sc[:, cs], acc_sc[:, cs] = online_update(s, vt_ref[:, off:off + nk], m_sc[:, cs], acc_sc[:, cs])

    def run(n_pairs):
        for w32, w16 in zip(w32_refs, w16_refs):
            w16[...] = w32[...].astype(BF16)
        scores(0, s0_sc)
        s = lax.dot_general(km_ref[...], qs, NT_DIMS, preferred_element_type=F32)
        m_sc[...], acc_sc[...] = online_update(s, vmt_ref[...], jnp.full(m_sc.shape, NEG_INF, F32),
                                               jnp.zeros(acc_sc.shape, F32))
        for jp in range(n_pairs):
            scores(2 * jp + 1, s1_sc)
            softmax_pv(2 * jp, s0_sc, None)
            scores(2 * jp + 2, s0_sc)
            softmax_pv(2 * jp + 1, s1_sc, None)

        scores(2 * n_pairs + 1, s1_sc)
        softmax_pv(2 * n_pairs, s0_sc, 0)
        softmax_pv(2 * n_pairs + 1, s1_sc, 1)

        lam = (jnp.exp(jnp.sum(lq1_ref[...] * lk1_ref[...], axis=-1, keepdims=True))
               - jnp.exp(jnp.sum(lq2_ref[...] * lk2_ref[...], axis=-1, keepdims=True)) + LAM_INIT)
        acc = acc_sc[...]
        o = acc[:HEAD_DIM] * (1.0 / acc[HEAD_DIM:HEAD_DIM + 1])
        od = o[:, :tq] - lam * o[:, tq:]
        ms = jnp.mean(od * od, axis=0, keepdims=True)
        on = od * lax.rsqrt(ms + EPS) * sg_ref[...] * (1.0 - LAM_INIT)
        o_ref[...] = on.T.astype(BF16)

    for n in range(SEQ // tq):
        pl.when(i == n)(functools.partial(run, n))


def _attention(q, k, vt, km, vmt, lq1, lk1, lq2, lk2, sg, cast_ws, *, batch, tq, cw):
    tk = tq // 2
    assert tk % cw == 0 and cw % CHUNK == 0
    nq = SEQ // tq
    steps = batch * N_HEADS * nq
    lspec = pl.BlockSpec((1, QK_DIM), lambda b, h, i: (0, 0))

    def slab(w):
        rep = next(r for r in (1, 2, 4, 8) if w.shape[0] * r % (BF16_ROWS * steps) == 0)
        return pl.BlockSpec((w.shape[0] * rep // steps, w.shape[1]),
                            lambda b, h, i: (((b * N_HEADS + h) * nq + i) // rep, 0))

    outs = pl.pallas_call(
        functools.partial(_attn_kernel, tq=tq, tk=tk, cw=cw, n_cast=len(cast_ws)),
        grid=(batch, N_HEADS, nq),
        in_specs=[
            pl.BlockSpec((tq, HEAD_DIM), lambda b, h, i: (b * nq + i, h)),
            pl.BlockSpec((SEQ, HEAD_DIM), lambda b, h, i: (b, h)),
            pl.BlockSpec((HEAD_DIM, SEQ), lambda b, h, i: (h, b)),
            pl.BlockSpec((N_META, HEAD_DIM), lambda b, h, i: (0, h)),
            pl.BlockSpec((HEAD_DIM, N_META), lambda b, h, i: (h, 0)),
            lspec, lspec, lspec, lspec,
            pl.BlockSpec((HEAD_DIM, 1), lambda b, h, i: (0, 0)),
        ] + [slab(w) for w in cast_ws],
        out_specs=[pl.BlockSpec((tq, HEAD_DIM), lambda b, h, i: (b * nq + i, h))] + [slab(w) for w in cast_ws],
        out_shape=[jax.ShapeDtypeStruct((batch * SEQ, ATTN_WIDTH), BF16)]
        + [jax.ShapeDtypeStruct(w.shape, BF16) for w in cast_ws],
        scratch_shapes=[
            pltpu.VMEM((1, 2 * tq), F32),
            pltpu.VMEM((HEAD_DIM + DENOM_ROWS, 2 * tq), F32),
            pltpu.VMEM((tk, 2 * tq), F32),
            pltpu.VMEM((tk, 2 * tq), F32),
        ],
        compiler_params=_params("parallel", "parallel", "arbitrary"),
        name="diff_attn",
    )(q, k, vt, km, vmt, lq1, lk1, lq2, lk2, sg, *cast_ws)
    return outs[0], outs[1:]


def _conv_kernel(u_ref, prev_ref, halo0_ref, w_ref, b_ref, g_ref, o_ref, win_sc, shift_sc, y_sc, *, tl, rc, rb):
    i = pl.program_id(1)

    @pl.when(i == 0)
    def _():
        win_sc[0:HALO, :] = halo0_ref[...]

    @pl.when(i > 0)
    def _():
        win_sc[0:HALO, :] = prev_ref[...]

    win_sc[HALO:HALO + tl, :] = u_ref[...]

    nsh = HALO + tl - SUBLANES
    for s in range(1, SUBLANES):
        shift_sc[s - 1] = win_sc[s:s + nsh, :]

    first = HALO - (CONV_KERNEL - 1)
    @pl.loop(0, CONV_WIDTH // LANES)
    def _(c):
        cs = pl.ds(pl.multiple_of(c * LANES, LANES), LANES)
        for r0 in range(0, tl // rc, rb):
            accs = [jnp.broadcast_to(b_ref[:, cs], (rc, LANES))] * rb
            for t in range(CONV_KERNEL):
                wt = w_ref[t:t + 1, cs]
                for k in range(rb):
                    lo = (r0 + k) * rc + first + t
                    s, al = lo % SUBLANES, lo - lo % SUBLANES
                    tap = win_sc[al:al + rc, cs] if s == 0 else shift_sc[s - 1, al:al + rc, cs]
                    accs[k] = accs[k] + wt * tap
            for k in range(rb):
                y_sc[(r0 + k) * rc:(r0 + k + 1) * rc, cs] = accs[k]

    yn = _rms(y_sc[...], g_ref[...])
    o_ref[...] = (yn * _sigmoid(yn)).astype(BF16)


def _conv(u, halo0, w, b, g, *, batch, tl, rc, rb):
    nt = SEQ // tl
    hb = tl // HALO
    return pl.pallas_call(
        functools.partial(_conv_kernel, tl=tl, rc=rc, rb=rb),
        grid=(batch, nt),
        in_specs=[
            pl.BlockSpec((tl, CONV_WIDTH), lambda bb, i: (bb * nt + i, 0)),
            pl.BlockSpec((HALO, CONV_WIDTH), lambda bb, i: (jnp.maximum((bb * nt + i) * hb - 1, 0), 0)),
            pl.BlockSpec((HALO, CONV_WIDTH), lambda bb, i: (0, 0)),
            pl.BlockSpec((CONV_KERNEL, CONV_WIDTH), lambda bb, i: (0, 0)),
            pl.BlockSpec((1, CONV_WIDTH), lambda bb, i: (0, 0)),
            pl.BlockSpec((1, CONV_WIDTH), lambda bb, i: (0, 0)),
        ],
        out_specs=pl.BlockSpec((tl, CONV_WIDTH), lambda bb, i: (bb * nt + i, 0)),
        out_shape=jax.ShapeDtypeStruct((batch * SEQ, CONV_WIDTH), BF16),
        scratch_shapes=[
            pltpu.VMEM((HALO + tl, CONV_WIDTH), F32),
            pltpu.VMEM((SUBLANES - 1, HALO + tl - SUBLANES, CONV_WIDTH), F32),
            pltpu.VMEM((tl, CONV_WIDTH), F32),
        ],
        compiler_params=_params("parallel", "arbitrary"),
        name="conv",
    )(u, u, halo0, w, b, g)


def _outproj_kernel(x_ref, a_ref, c_ref, wa_ref, wc_ref, o_ref):
    o_ref[...] = (x_ref[...]
                  + jnp.dot(a_ref[...], wa_ref[...].astype(BF16), preferred_element_type=F32)
                  + jnp.dot(c_ref[...], wc_ref[...].astype(BF16), preferred_element_type=F32))


def _out_proj(x, oa, oc, w_out, *, tm, tn):
    m = x.shape[0]
    return pl.pallas_call(
        _outproj_kernel,
        grid=(m // tm, D_MODEL // tn),
        in_specs=[
            pl.BlockSpec((tm, tn), lambda i, j: (i, j)),
            pl.BlockSpec((tm, ATTN_WIDTH), lambda i, j: (i, 0)),
            pl.BlockSpec((tm, CONV_WIDTH), lambda i, j: (i, 0)),
            pl.BlockSpec((ATTN_WIDTH, tn), lambda i, j: (0, j)),
            pl.BlockSpec((CONV_WIDTH, tn), lambda i, j: (1, j)),
        ],
        out_specs=pl.BlockSpec((tm, tn), lambda i, j: (i, j)),
        out_shape=jax.ShapeDtypeStruct((m, D_MODEL), F32),
        compiler_params=_params("parallel", "arbitrary"),
        name="out_proj",
    )(x, oa, oc, w_out, w_out)


def kernel(x, meta_tokens, ffn1_norm_g, ffn1_w_gate, ffn1_w_up, ffn1_w_down, mix_norm_g, w_in, q_norm_g,
           k_norm_g, lambda_q1, lambda_k1, lambda_q2, lambda_k2, attn_subln_g, conv_w, conv_b, conv_norm_g,
           w_out, ffn2_norm_g, ffn2_w_gate, ffn2_w_up, ffn2_w_down, final_norm_g):
    batch = x.shape[0]
    xr = x.reshape(batch * SEQ, D_MODEL)

    w1g, w1u, w1d = ffn1_w_gate[0], ffn1_w_up[0], ffn1_w_down[0]
    w2g, w2u, w2d = ffn2_w_gate[0], ffn2_w_up[0], ffn2_w_down[0]
    win, wout = w_in[0], w_out[0]
    qg = jnp.tile(q_norm_g[0] * (QK_DIM ** -0.5 * LOG2E), 2)[None, :]
    kg = jnp.tile(k_norm_g[0], 2)[None, :]
    sg = attn_subln_g[0][:, None]
    cw = conv_w[0].reshape(CONV_KERNEL, CONV_WIDTH)

    ffn = functools.partial(_ffn, tm=1024, tf=256)
    m1 = _ffn(meta_tokens, ffn1_norm_g, w1g, w1u, w1d, final_norm_g, tm=N_META, tf=512, final_norm=False)
    _, km, vmt, um = _in_proj(m1, mix_norm_g, win, qg, kg, tm=N_META, tn=512)
    halo0 = jnp.concatenate([jnp.zeros((HALO - N_META, CONV_WIDTH), F32), um], axis=0)

    x1 = ffn(xr, ffn1_norm_g, w1g, w1u, w1d, final_norm_g, final_norm=False)
    q, k, vt, u = _in_proj(x1, mix_norm_g, win, qg, kg, tm=1024, tn=256)
    oa, (w2g, w2u, w2d, wout) = _attention(q, k, vt, km, vmt, lambda_q1, lambda_k1, lambda_q2, lambda_k2, sg,
                                           [w2g, w2u, w2d, wout], batch=batch, tq=1024, cw=256)
    oc = _conv(u, halo0, cw, conv_b, conv_norm_g, batch=batch, tl=256, rc=32, rb=8)
    x2 = _out_proj(x1, oa, oc, wout, tm=512, tn=D_MODEL)
    y = _ffn(x2, ffn2_norm_g, w2g, w2u, w2d, final_norm_g, tm=1024, tf=512, final_norm=True)
    return y.reshape(batch, SEQ, D_MODEL)
```

```python
import functools
import math

import jax
import jax.numpy as jnp
from jax import lax
from jax.experimental import pallas as pl
from jax.experimental.pallas import tpu as pltpu

F32 = jnp.float32
BF16 = jnp.bfloat16

D_MODEL = 2048
SEQ = 4096
N_META = 16
CHUNK = 64
ATTN_WIDTH = 1024
CONV_WIDTH = 1024
HEAD_DIM = 128
N_HEADS = ATTN_WIDTH // HEAD_DIM
QK_DIM = HEAD_DIM // 2
CONV_KERNEL = 31
D_FF = 5632
EPS = 1e-6
NEG_INF = -1e30
LAM_INIT = 0.8 - 0.6 * math.exp(-0.3 * 1)

LOG2E = math.log2(math.e)

LANES = 128
SUBLANES = 8
BF16_ROWS = 16
DENOM_ROWS = 16
HALO = 32
VMEM_LIMIT = 56 * 1024 * 1024

NT_DIMS = (((1,), (1,)), ((), ()))


def _params(*sem):
    return pltpu.CompilerParams(dimension_semantics=sem, vmem_limit_bytes=VMEM_LIMIT)


def _rms(x, g):
    ms = jnp.mean(x * x, axis=-1, keepdims=True)
    return x * lax.rsqrt(ms + EPS) * g


def _sigmoid(x):
    return 1.0 / (1.0 + jnp.exp(-x))


def _ffn_kernel(x_ref, g_ref, wg_ref, wu_ref, wd_ref, fg_ref, o_ref, *rest, final_norm, emit_w16):
    h_ref = rest[-1]
    j = pl.program_id(1)

    @pl.when(j == 0)
    def _():
        x = x_ref[...]
        h_ref[...] = _rms(x, g_ref[...]).astype(BF16)
        o_ref[...] = x

    wg, wu, wd = (w_ref[...].astype(BF16) for w_ref in (wg_ref, wu_ref, wd_ref))
    if emit_w16:
        for w16_ref, w in zip(rest[:3], (wg, wu, wd)):
            w16_ref[...] = w
    h = h_ref[...]
    gate = jnp.dot(h, wg, preferred_element_type=F32)
    up = jnp.dot(h, wu, preferred_element_type=F32)
    a = (gate * _sigmoid(gate)) * (up * 0.5)
    o_ref[...] += jnp.dot(a.astype(BF16), wd, preferred_element_type=F32)

    if final_norm:
        @pl.when(j == pl.num_programs(1) - 1)
        def _():
            o_ref[...] = _rms(o_ref[...], fg_ref[...])


def _ffn(x, g, wg, wu, wd, fg, *, tm, tf, final_norm, emit_w16=False):
    m = x.shape[0]
    grid = (m // tm, D_FF // tf)
    assert not emit_w16 or grid[0] == 1
    wspecs = [
        pl.BlockSpec((D_MODEL, tf), lambda i, j: (0, j)),
        pl.BlockSpec((D_MODEL, tf), lambda i, j: (0, j)),
        pl.BlockSpec((tf, D_MODEL), lambda i, j: (j, 0)),
    ]
    out_specs = [pl.BlockSpec((tm, D_MODEL), lambda i, j: (i, 0))]
    out_shape = [jax.ShapeDtypeStruct((m, D_MODEL), F32)]
    if emit_w16:
        out_specs += wspecs
        out_shape += [jax.ShapeDtypeStruct(w.shape, BF16) for w in (wg, wu, wd)]
    outs = pl.pallas_call(
        functools.partial(_ffn_kernel, final_norm=final_norm, emit_w16=emit_w16),
        grid=grid,
        in_specs=[
            pl.BlockSpec((tm, D_MODEL), lambda i, j: (i, 0)),
            pl.BlockSpec((1, D_MODEL), lambda i, j: (0, 0)),
            *wspecs,
            pl.BlockSpec((1, D_MODEL), lambda i, j: (0, 0)),
        ],
        out_specs=out_specs,
        out_shape=out_shape,
        scratch_shapes=[pltpu.VMEM((tm, D_MODEL), BF16)],
        compiler_params=_params("parallel", "arbitrary"),
        name="ffn_final" if final_norm else "ffn",
    )(x, g, wg, wu, wd, fg)
    return (outs[0], outs[1:]) if emit_w16 else outs[0]


def _qk_norm(x, g):
    outs = []
    for hh in range(x.shape[1] // LANES):
        xh = x[:, hh * LANES:(hh + 1) * LANES]
        lo = lax.broadcasted_iota(jnp.int32, xh.shape, 1) < QK_DIM
        x2 = xh * xh
        s_lo = jnp.sum(jnp.where(lo, x2, 0.0), axis=-1, keepdims=True)
        s_hi = jnp.sum(jnp.where(lo, 0.0, x2), axis=-1, keepdims=True)
        ms = jnp.where(lo, s_lo, s_hi) * (1.0 / QK_DIM)
        outs.append(xh * lax.rsqrt(ms + EPS) * g)
    return outs[0] if len(outs) == 1 else jnp.concatenate(outs, axis=1)


def _inproj_kernel(x_ref, g_ref, wq_ref, wk_ref, wv_ref, wa_ref, wc_ref, qg_ref, kg_ref,
                   q_ref, k_ref, vt_ref, u_ref, *rest, emit_w16):
    h_ref = rest[-1]
    j = pl.program_id(1)

    @pl.when(j == 0)
    def _():
        h_ref[...] = _rms(x_ref[...], g_ref[...]).astype(BF16)

    ws = [w_ref[...].astype(BF16) for w_ref in (wq_ref, wk_ref, wv_ref, wa_ref, wc_ref)]
    if emit_w16:
        for w16_ref, w in zip(rest[:5], ws):
            w16_ref[...] = w
    h = h_ref[...]
    q, k, v, ca, cg = (jnp.dot(h, w, preferred_element_type=F32) for w in ws)
    q_ref[...] = _qk_norm(q, qg_ref[...]).astype(BF16)
    k_ref[...] = _qk_norm(k, kg_ref[...]).astype(BF16)
    vt_ref[...] = v.T.astype(BF16)
    u_ref[...] = ca * _sigmoid(cg)


def _in_proj(x, g, ws, qg, kg, *, tm, tn, emit_w16=False):
    m = x.shape[0]
    nb = ATTN_WIDTH // tn
    assert not emit_w16 or m == tm
    if isinstance(ws, (tuple, list)):
        wspecs = [pl.BlockSpec((D_MODEL, tn), lambda i, j: (0, j))] * 5
    else:
        wspecs = [pl.BlockSpec((D_MODEL, tn), lambda i, j, c=c: (0, c * nb + j)) for c in range(5)]
        ws = [ws] * 5
    out_specs = [
        pl.BlockSpec((tm, tn), lambda i, j: (i, j)),
        pl.BlockSpec((tm, tn), lambda i, j: (i, j)),
        pl.BlockSpec((tn, tm), lambda i, j: (j, i)),
        pl.BlockSpec((tm, tn), lambda i, j: (i, j)),
    ]
    out_shape = [
        jax.ShapeDtypeStruct((m, ATTN_WIDTH), BF16),
        jax.ShapeDtypeStruct((m, ATTN_WIDTH), BF16),
        jax.ShapeDtypeStruct((ATTN_WIDTH, m), BF16),
        jax.ShapeDtypeStruct((m, CONV_WIDTH), F32),
    ]
    if emit_w16:
        out_specs += [pl.BlockSpec((D_MODEL, tn), lambda i, j: (0, j))] * 5
        out_shape += [jax.ShapeDtypeStruct((D_MODEL, ATTN_WIDTH), BF16)] * 5
    outs = pl.pallas_call(
        functools.partial(_inproj_kernel, emit_w16=emit_w16),
        grid=(m // tm, nb),
        in_specs=[
            pl.BlockSpec((tm, D_MODEL), lambda i, j: (i, 0)),
            pl.BlockSpec((1, D_MODEL), lambda i, j: (0, 0)),
            *wspecs,
            pl.BlockSpec((1, LANES), lambda i, j: (0, 0)),
            pl.BlockSpec((1, LANES), lambda i, j: (0, 0)),
        ],
        out_specs=out_specs,
        out_shape=out_shape,
        scratch_shapes=[pltpu.VMEM((tm, D_MODEL), BF16)],
        compiler_params=_params("parallel", "arbitrary"),
        name="in_proj",
    )(x, g, *ws, qg, kg)
    return (outs[:4], outs[4:]) if emit_w16 else outs


def _attn_kernel(q_ref, k_ref, vt_ref, km_ref, vmt_ref, lq1_ref, lk1_ref, lq2_ref, lk2_ref, sg_ref,
                 *rest, tq, tk, cw, n_cast):
    w32_refs, o_ref, w16_refs = rest[:n_cast], rest[n_cast], rest[n_cast + 1:2 * n_cast + 1]
    m_sc, acc_sc, s0_sc, s1_sc = rest[2 * n_cast + 1:]
    i = pl.program_id(2)
    q = q_ref[...]
    lane = lax.broadcasted_iota(jnp.int32, q.shape, 1)
    zero = jnp.zeros_like(q)
    qs = jnp.concatenate([jnp.where(lane < QK_DIM, q, zero),
                          jnp.where(lane < QK_DIM, zero, q)], axis=0)

    def scores(blk, s_ref):
        s_ref[...] = lax.dot_general(k_ref[blk * tk:(blk + 1) * tk, :], qs, NT_DIMS,
                                     preferred_element_type=F32)

    def with_ones(vt):
        return jnp.concatenate([vt, jnp.ones((DENOM_ROWS, vt.shape[1]), BF16)], axis=0)

    def online_update(s, vt, m_old, acc_old):
        m_new = jnp.maximum(m_old, jnp.max(s, axis=0, keepdims=True))
        alpha = jnp.exp2(m_old - m_new)
        p = jnp.exp2(s - m_new)
        return m_new, alpha * acc_old + jnp.dot(with_ones(vt), p.astype(BF16), preferred_element_type=F32)

    def softmax_pv(blk, s_ref, diag):
        off = blk * tk
        for c in range(2 * tq // cw):
            cs = slice(c * cw, (c + 1) * cw)
            nk, masked = tk, False
            if diag is not None:
                q0 = (c * cw) % tq
                nk = min(tk, q0 + cw - diag * tk)
                if nk <= 0:
                    continue
                masked = (diag * tk + nk - 1) // CHUNK > q0 // CHUNK
            s = s_ref[0:nk, cs]
            if masked:
                kchunk = (lax.broadcasted_iota(jnp.int32, (nk, cw), 0) + diag * tk) // CHUNK
                qchunk = (lax.broadcasted_iota(jnp.int32, (nk, cw), 1) + q0) // CHUNK
                s = jnp.where(kchunk <= qchunk, s, NEG_INF)
            m_sc[:, cs], acc_sc[:, cs] = online_update(s, vt_ref[:, off:off + nk], m_sc[:, cs], acc_sc[:, cs])

    def run(n_pairs):
        for w32, w16 in zip(w32_refs, w16_refs):
            w16[...] = w32[...].astype(BF16)
        scores(0, s0_sc)
        s = lax.dot_general(km_ref[...], qs, NT_DIMS, preferred_element_type=F32)
        m_sc[...], acc_sc[...] = online_update(s, vmt_ref[...], jnp.full(m_sc.shape, NEG_INF, F32),
                                               jnp.zeros(acc_sc.shape, F32))
        for jp in range(n_pairs):
            scores(2 * jp + 1, s1_sc)
            softmax_pv(2 * jp, s0_sc, None)
            scores(2 * jp + 2, s0_sc)
            softmax_pv(2 * jp + 1, s1_sc, None)

        scores(2 * n_pairs + 1, s1_sc)
        softmax_pv(2 * n_pairs, s0_sc, 0)
        softmax_pv(2 * n_pairs + 1, s1_sc, 1)

        lam = (jnp.exp(jnp.sum(lq1_ref[...] * lk1_ref[...], axis=-1, keepdims=True))
               - jnp.exp(jnp.sum(lq2_ref[...] * lk2_ref[...], axis=-1, keepdims=True)) + LAM_INIT)
        acc = acc_sc[...]
        o = acc[:HEAD_DIM] * (1.0 / acc[HEAD_DIM:HEAD_DIM + 1])
        od = o[:, :tq] - lam * o[:, tq:]
        ms = jnp.mean(od * od, axis=0, keepdims=True)
        on = od * lax.rsqrt(ms + EPS) * sg_ref[...] * (1.0 - LAM_INIT)
        o_ref[...] = on.T.astype(BF16)

    for n in range(SEQ // tq):
        pl.when(i == n)(functools.partial(run, n))


def _attention(q, k, vt, km, vmt, lq1, lk1, lq2, lk2, sg, cast_ws, *, batch, tq, cw):
    tk = tq // 2
    assert tk % cw == 0 and cw % CHUNK == 0
    nq = SEQ // tq
    steps = batch * N_HEADS * nq
    lspec = pl.BlockSpec((1, QK_DIM), lambda b, h, i: (0, 0))

    def slab(w):
        rep = next(r for r in (1, 2, 4, 8) if w.shape[0] * r % (BF16_ROWS * steps) == 0)
        return pl.BlockSpec((w.shape[0] * rep // steps, w.shape[1]),
                            lambda b, h, i: (((b * N_HEADS + h) * nq + i) // rep, 0))

    outs = pl.pallas_call(
        functools.partial(_attn_kernel, tq=tq, tk=tk, cw=cw, n_cast=len(cast_ws)),
        grid=(batch, N_HEADS, nq),
        in_specs=[
            pl.BlockSpec((tq, HEAD_DIM), lambda b, h, i: (b * nq + i, h)),
            pl.BlockSpec((SEQ, HEAD_DIM), lambda b, h, i: (b, h)),
            pl.BlockSpec((HEAD_DIM, SEQ), lambda b, h, i: (h, b)),
            pl.BlockSpec((N_META, HEAD_DIM), lambda b, h, i: (0, h)),
            pl.BlockSpec((HEAD_DIM, N_META), lambda b, h, i: (h, 0)),
            lspec, lspec, lspec, lspec,
            pl.BlockSpec((HEAD_DIM, 1), lambda b, h, i: (0, 0)),
        ] + [slab(w) for w in cast_ws],
        out_specs=[pl.BlockSpec((tq, HEAD_DIM), lambda b, h, i: (b * nq + i, h))] + [slab(w) for w in cast_ws],
        out_shape=[jax.ShapeDtypeStruct((batch * SEQ, ATTN_WIDTH), BF16)]
        + [jax.ShapeDtypeStruct(w.shape, BF16) for w in cast_ws],
        scratch_shapes=[
            pltpu.VMEM((1, 2 * tq), F32),
            pltpu.VMEM((HEAD_DIM + DENOM_ROWS, 2 * tq), F32),
            pltpu.VMEM((tk, 2 * tq), F32),
            pltpu.VMEM((tk, 2 * tq), F32),
        ],
        compiler_params=_params("parallel", "parallel", "arbitrary"),
        name="diff_attn",
    )(q, k, vt, km, vmt, lq1, lk1, lq2, lk2, sg, *cast_ws)
    return outs[0], outs[1:]


def _conv_kernel(u_ref, prev_ref, halo0_ref, w_ref, b_ref, g_ref, o_ref, win_sc, shift_sc, y_sc, *, tl, rc, rb):
    i = pl.program_id(1)

    @pl.when(i == 0)
    def _():
        win_sc[0:HALO, :] = halo0_ref[...]

    @pl.when(i > 0)
    def _():
        win_sc[0:HALO, :] = prev_ref[...]

    win_sc[HALO:HALO + tl, :] = u_ref[...]

    nsh = HALO + tl - SUBLANES
    for s in range(1, SUBLANES):
        shift_sc[s - 1] = win_sc[s:s + nsh, :]

    first = HALO - (CONV_KERNEL - 1)
    @pl.loop(0, CONV_WIDTH // LANES)
    def _(c):
        cs = pl.ds(pl.multiple_of(c * LANES, LANES), LANES)
        for r0 in range(0, tl // rc, rb):
            accs = [jnp.broadcast_to(b_ref[:, cs], (rc, LANES))] * rb
            for t in range(CONV_KERNEL):
                wt = w_ref[t:t + 1, cs]
                for k in range(rb):
                    lo = (r0 + k) * rc + first + t
                    s, al = lo % SUBLANES, lo - lo % SUBLANES
                    tap = win_sc[al:al + rc, cs] if s == 0 else shift_sc[s - 1, al:al + rc, cs]
                    accs[k] = accs[k] + wt * tap
            for k in range(rb):
                y_sc[(r0 + k) * rc:(r0 + k + 1) * rc, cs] = accs[k]

    yn = _rms(y_sc[...], g_ref[...])
    o_ref[...] = (yn * _sigmoid(yn)).astype(BF16)


def _conv(u, halo0, w, b, g, *, batch, tl, rc, rb):
    nt = SEQ // tl
    hb = tl // HALO
    return pl.pallas_call(
        functools.partial(_conv_kernel, tl=tl, rc=rc, rb=rb),
        grid=(batch, nt),
        in_specs=[
            pl.BlockSpec((tl, CONV_WIDTH), lambda bb, i: (bb * nt + i, 0)),
            pl.BlockSpec((HALO, CONV_WIDTH), lambda bb, i: (jnp.maximum((bb * nt + i) * hb - 1, 0), 0)),
            pl.BlockSpec((HALO, CONV_WIDTH), lambda bb, i: (0, 0)),
            pl.BlockSpec((CONV_KERNEL, CONV_WIDTH), lambda bb, i: (0, 0)),
            pl.BlockSpec((1, CONV_WIDTH), lambda bb, i: (0, 0)),
            pl.BlockSpec((1, CONV_WIDTH), lambda bb, i: (0, 0)),
        ],
        out_specs=pl.BlockSpec((tl, CONV_WIDTH), lambda bb, i: (bb * nt + i, 0)),
        out_shape=jax.ShapeDtypeStruct((batch * SEQ, CONV_WIDTH), BF16),
        scratch_shapes=[
            pltpu.VMEM((HALO + tl, CONV_WIDTH), F32),
            pltpu.VMEM((SUBLANES - 1, HALO + tl - SUBLANES, CONV_WIDTH), F32),
            pltpu.VMEM((tl, CONV_WIDTH), F32),
        ],
        compiler_params=_params("parallel", "arbitrary"),
        name="conv",
    )(u, u, halo0, w, b, g)


def _outproj_kernel(x_ref, a_ref, c_ref, wa_ref, wc_ref, o_ref):
    o_ref[...] = (x_ref[...]
                  + jnp.dot(a_ref[...], wa_ref[...].astype(BF16), preferred_element_type=F32)
                  + jnp.dot(c_ref[...], wc_ref[...].astype(BF16), preferred_element_type=F32))


def _out_proj(x, oa, oc, w_out, *, tm, tn):
    m = x.shape[0]
    return pl.pallas_call(
        _outproj_kernel,
        grid=(m // tm, D_MODEL // tn),
        in_specs=[
            pl.BlockSpec((tm, tn), lambda i, j: (i, j)),
            pl.BlockSpec((tm, ATTN_WIDTH), lambda i, j: (i, 0)),
            pl.BlockSpec((tm, CONV_WIDTH), lambda i, j: (i, 0)),
            pl.BlockSpec((ATTN_WIDTH, tn), lambda i, j: (0, j)),
            pl.BlockSpec((CONV_WIDTH, tn), lambda i, j: (1, j)),
        ],
        out_specs=pl.BlockSpec((tm, tn), lambda i, j: (i, j)),
        out_shape=jax.ShapeDtypeStruct((m, D_MODEL), F32),
        compiler_params=_params("parallel", "arbitrary"),
        name="out_proj",
    )(x, oa, oc, w_out, w_out)


def kernel(x, meta_tokens, ffn1_norm_g, ffn1_w_gate, ffn1_w_up, ffn1_w_down, mix_norm_g, w_in, q_norm_g,
           k_norm_g, lambda_q1, lambda_k1, lambda_q2, lambda_k2, attn_subln_g, conv_w, conv_b, conv_norm_g,
           w_out, ffn2_norm_g, ffn2_w_gate, ffn2_w_up, ffn2_w_down, final_norm_g):
    batch = x.shape[0]
    xr = x.reshape(batch * SEQ, D_MODEL)

    w1g, w1u, w1d = ffn1_w_gate[0], ffn1_w_up[0], ffn1_w_down[0]
    w2g, w2u, w2d = ffn2_w_gate[0], ffn2_w_up[0], ffn2_w_down[0]
    win, wout = w_in[0], w_out[0]
    qg = jnp.tile(q_norm_g[0] * (QK_DIM ** -0.5 * LOG2E), 2)[None, :]
    kg = jnp.tile(k_norm_g[0], 2)[None, :]
    sg = attn_subln_g[0][:, None]
    cw = conv_w[0].reshape(CONV_KERNEL, CONV_WIDTH)

    m1, (w1g, w1u, w1d) = _ffn(meta_tokens, ffn1_norm_g, w1g, w1u, w1d, final_norm_g,
                               tm=N_META, tf=512, final_norm=False, emit_w16=True)
    (_, km, vmt, um), win16 = _in_proj(m1, mix_norm_g, win, qg, kg, tm=N_META, tn=256, emit_w16=True)
    halo0 = jnp.concatenate([jnp.zeros((HALO - N_META, CONV_WIDTH), F32), um], axis=0)

    x1 = _ffn(xr, ffn1_norm_g, w1g, w1u, w1d, final_norm_g, tm=1024, tf=512, final_norm=False)
    q, k, vt, u = _in_proj(x1, mix_norm_g, win16, qg, kg, tm=1024, tn=256)
    oa, (w2g, w2u, w2d, wout) = _attention(q, k, vt, km, vmt, lambda_q1, lambda_k1, lambda_q2, lambda_k2, sg,
                                           [w2g, w2u, w2d, wout], batch=batch, tq=1024, cw=256)
    oc = _conv(u, halo0, cw, conv_b, conv_norm_g, batch=batch, tl=256, rc=32, rb=8)
    x2 = _out_proj(x1, oa, oc, wout, tm=512, tn=D_MODEL)
    y = _ffn(x2, ffn2_norm_g, w2g, w2u, w2d, final_norm_g, tm=1024, tf=512, final_norm=True)
    return y.reshape(batch, SEQ, D_MODEL)
```

```python
import functools
import math

import jax
import jax.numpy as jnp
from jax import lax
from jax.experimental import pallas as pl
from jax.experimental.pallas import tpu as pltpu

F32 = jnp.float32
BF16 = jnp.bfloat16

D_MODEL = 2048
SEQ = 4096
N_META = 16
CHUNK = 64
ATTN_WIDTH = 1024
CONV_WIDTH = 1024
HEAD_DIM = 128
N_HEADS = ATTN_WIDTH // HEAD_DIM
QK_DIM = HEAD_DIM // 2
CONV_KERNEL = 31
D_FF = 5632
EPS = 1e-6
NEG_INF = -1e30
LAM_INIT = 0.8 - 0.6 * math.exp(-0.3 * 1)

LOG2E = math.log2(math.e)

LANES = 128
SUBLANES = 8
BF16_ROWS = 16
DENOM_ROWS = 16
HALO = 32
VMEM_LIMIT = 56 * 1024 * 1024


def _params(*sem):
    return pltpu.CompilerParams(dimension_semantics=sem, vmem_limit_bytes=VMEM_LIMIT)


def _rms(x, g):
    ms = jnp.mean(x * x, axis=-1, keepdims=True)
    return x * lax.rsqrt(ms + EPS) * g


def _sigmoid(x):
    return 1.0 / (1.0 + jnp.exp(-x))


def _ffn_kernel(x_ref, g_ref, wg_ref, wu_ref, wd_ref, fg_ref, o_ref, *rest, final_norm, emit_w16):
    h_ref = rest[-1]
    j = pl.program_id(1)

    @pl.when(j == 0)
    def _():
        x = x_ref[...]
        h_ref[...] = _rms(x, g_ref[...]).astype(BF16)
        o_ref[...] = x

    wg, wu, wd = (w_ref[...].astype(BF16) for w_ref in (wg_ref, wu_ref, wd_ref))
    if emit_w16:
        for w16_ref, w in zip(rest[:3], (wg, wu, wd)):
            w16_ref[...] = w
    h = h_ref[...]
    gate = jnp.dot(h, wg, preferred_element_type=F32)
    up = jnp.dot(h, wu, preferred_element_type=F32)
    a = (gate * _sigmoid(gate)) * (up * 0.5)
    o_ref[...] += jnp.dot(a.astype(BF16), wd, preferred_element_type=F32)

    if final_norm:
        @pl.when(j == pl.num_programs(1) - 1)
        def _():
            o_ref[...] = _rms(o_ref[...], fg_ref[...])


def _ffn(x, g, wg, wu, wd, fg, *, tm, tf, final_norm, emit_w16=False):
    m = x.shape[0]
    grid = (m // tm, D_FF // tf)
    assert not emit_w16 or grid[0] == 1
    wspecs = [
        pl.BlockSpec((D_MODEL, tf), lambda i, j: (0, j)),
        pl.BlockSpec((D_MODEL, tf), lambda i, j: (0, j)),
        pl.BlockSpec((tf, D_MODEL), lambda i, j: (j, 0)),
    ]
    out_specs = [pl.BlockSpec((tm, D_MODEL), lambda i, j: (i, 0))]
    out_shape = [jax.ShapeDtypeStruct((m, D_MODEL), F32)]
    if emit_w16:
        out_specs += wspecs
        out_shape += [jax.ShapeDtypeStruct(w.shape, BF16) for w in (wg, wu, wd)]
    outs = pl.pallas_call(
        functools.partial(_ffn_kernel, final_norm=final_norm, emit_w16=emit_w16),
        grid=grid,
        in_specs=[
            pl.BlockSpec((tm, D_MODEL), lambda i, j: (i, 0)),
            pl.BlockSpec((1, D_MODEL), lambda i, j: (0, 0)),
            *wspecs,
            pl.BlockSpec((1, D_MODEL), lambda i, j: (0, 0)),
        ],
        out_specs=out_specs,
        out_shape=out_shape,
        scratch_shapes=[pltpu.VMEM((tm, D_MODEL), BF16)],
        compiler_params=_params("parallel", "arbitrary"),
        name="ffn_final" if final_norm else "ffn",
    )(x, g, wg, wu, wd, fg)
    return (outs[0], outs[1:]) if emit_w16 else outs[0]


def _qk_norm(x, g):
    outs = []
    for hh in range(x.shape[1] // LANES):
        xh = x[:, hh * LANES:(hh + 1) * LANES]
        lo = lax.broadcasted_iota(jnp.int32, xh.shape, 1) < QK_DIM
        x2 = xh * xh
        s_lo = jnp.sum(jnp.where(lo, x2, 0.0), axis=-1, keepdims=True)
        s_hi = jnp.sum(jnp.where(lo, 0.0, x2), axis=-1, keepdims=True)
        ms = jnp.where(lo, s_lo, s_hi) * (1.0 / QK_DIM)
        outs.append(xh * lax.rsqrt(ms + EPS) * g)
    return outs[0] if len(outs) == 1 else jnp.concatenate(outs, axis=1)


def _inproj_kernel(x_ref, g_ref, wq_ref, wk_ref, wv_ref, wa_ref, wc_ref, qg_ref, kg_ref,
                   q_ref, k_ref, vt_ref, u_ref, *rest, emit_w16):
    h_ref = rest[-1]
    j = pl.program_id(1)

    @pl.when(j == 0)
    def _():
        h_ref[...] = _rms(x_ref[...], g_ref[...]).astype(BF16)

    ws = [w_ref[...].astype(BF16) for w_ref in (wq_ref, wk_ref, wv_ref, wa_ref, wc_ref)]
    if emit_w16:
        for w16_ref, w in zip(rest[:5], ws):
            w16_ref[...] = w
    h = h_ref[...]
    q, k, v, ca, cg = (jnp.dot(h, w, preferred_element_type=F32) for w in ws)
    q_ref[...] = _qk_norm(q, qg_ref[...]).astype(BF16)
    k_ref[...] = _qk_norm(k, kg_ref[...]).astype(BF16)
    vt_ref[...] = v.T.astype(BF16)
    u_ref[...] = ca * _sigmoid(cg)


def _in_proj(x, g, ws, qg, kg, *, tm, tn, emit_w16=False):
    m = x.shape[0]
    nb = ATTN_WIDTH // tn
    assert not emit_w16 or m == tm
    if isinstance(ws, (tuple, list)):
        wspecs = [pl.BlockSpec((D_MODEL, tn), lambda i, j: (0, j))] * 5
    else:
        wspecs = [pl.BlockSpec((D_MODEL, tn), lambda i, j, c=c: (0, c * nb + j)) for c in range(5)]
        ws = [ws] * 5
    out_specs = [
        pl.BlockSpec((tm, tn), lambda i, j: (i, j)),
        pl.BlockSpec((tm, tn), lambda i, j: (i, j)),
        pl.BlockSpec((tn, tm), lambda i, j: (j, i)),
        pl.BlockSpec((tm, tn), lambda i, j: (i, j)),
    ]
    out_shape = [
        jax.ShapeDtypeStruct((m, ATTN_WIDTH), BF16),
        jax.ShapeDtypeStruct((m, ATTN_WIDTH), BF16),
        jax.ShapeDtypeStruct((ATTN_WIDTH, m), BF16),
        jax.ShapeDtypeStruct((m, CONV_WIDTH), F32),
    ]
    if emit_w16:
        out_specs += [pl.BlockSpec((D_MODEL, tn), lambda i, j: (0, j))] * 5
        out_shape += [jax.ShapeDtypeStruct((D_MODEL, ATTN_WIDTH), BF16)] * 5
    outs = pl.pallas_call(
        functools.partial(_inproj_kernel, emit_w16=emit_w16),
        grid=(m // tm, nb),
        in_specs=[
            pl.BlockSpec((tm, D_MODEL), lambda i, j: (i, 0)),
            pl.BlockSpec((1, D_MODEL), lambda i, j: (0, 0)),
            *wspecs,
            pl.BlockSpec((1, LANES), lambda i, j: (0, 0)),
            pl.BlockSpec((1, LANES), lambda i, j: (0, 0)),
        ],
        out_specs=out_specs,
        out_shape=out_shape,
        scratch_shapes=[pltpu.VMEM((tm, D_MODEL), BF16)],
        compiler_params=_params("parallel", "arbitrary"),
        name="in_proj",
    )(x, g, *ws, qg, kg)
    return (outs[:4], outs[4:]) if emit_w16 else outs


def _attn_kernel(q_ref, k_ref, vt_ref, km_ref, vmt_ref, lq1_ref, lk1_ref, lq2_ref, lk2_ref, sg_ref,
                 *rest, tq, tk, cw, n_cast):
    w32_refs, o_ref, w16_refs = rest[:n_cast], rest[n_cast], rest[n_cast + 1:2 * n_cast + 1]
    m_sc, acc_sc, s0_sc, s1_sc = rest[2 * n_cast + 1:]
    i = pl.program_id(2)
    qt = q_ref[...].astype(F32).T
    row = lax.broadcasted_iota(jnp.int32, qt.shape, 0)
    qs = jnp.concatenate([jnp.where(row < QK_DIM, qt, 0.0),
                          jnp.where(row < QK_DIM, 0.0, qt)], axis=1).astype(BF16)

    def scores(blk, s_ref):
        s_ref[...] = jnp.dot(k_ref[blk * tk:(blk + 1) * tk, :], qs, preferred_element_type=F32)

    def with_ones(vt):
        return jnp.concatenate([vt, jnp.ones((DENOM_ROWS, vt.shape[1]), BF16)], axis=0)

    def online_update(s, vt, m_old, acc_old):
        m_new = jnp.maximum(m_old, jnp.max(s, axis=0, keepdims=True))
        alpha = jnp.exp2(m_old - m_new)
        p = jnp.exp2(s - m_new)
        return m_new, alpha * acc_old + jnp.dot(with_ones(vt), p.astype(BF16), preferred_element_type=F32)

    def softmax_pv(blk, s_ref, diag):
        off = blk * tk
        for c in range(2 * tq // cw):
            cs = slice(c * cw, (c + 1) * cw)
            nk, masked = tk, False
            if diag is not None:
                q0 = (c * cw) % tq
                nk = min(tk, q0 + cw - diag * tk)
                if nk <= 0:
                    continue
                masked = (diag * tk + nk - 1) // CHUNK > q0 // CHUNK
            s = s_ref[0:nk, cs]
            if masked:
                kchunk = (lax.broadcasted_iota(jnp.int32, (nk, cw), 0) + diag * tk) // CHUNK
                qchunk = (lax.broadcasted_iota(jnp.int32, (nk, cw), 1) + q0) // CHUNK
                s = jnp.where(kchunk <= qchunk, s, NEG_INF)
            m_sc[:, cs], acc_sc[:, cs] = online_update(s, vt_ref[:, off:off + nk], m_sc[:, cs], acc_sc[:, cs])

    def run(n_pairs):
        for w32, w16 in zip(w32_refs, w16_refs):
            w16[...] = w32[...].astype(BF16)
        scores(0, s0_sc)
        s = jnp.dot(km_ref[...], qs, preferred_element_type=F32)
        m_sc[...], acc_sc[...] = online_update(s, vmt_ref[...], jnp.full(m_sc.shape, NEG_INF, F32),
                                               jnp.zeros(acc_sc.shape, F32))
        for jp in range(n_pairs):
            scores(2 * jp + 1, s1_sc)
            softmax_pv(2 * jp, s0_sc, None)
            scores(2 * jp + 2, s0_sc)
            softmax_pv(2 * jp + 1, s1_sc, None)

        scores(2 * n_pairs + 1, s1_sc)
        softmax_pv(2 * n_pairs, s0_sc, 0)
        softmax_pv(2 * n_pairs + 1, s1_sc, 1)

        lam = (jnp.exp(jnp.sum(lq1_ref[...] * lk1_ref[...], axis=-1, keepdims=True))
               - jnp.exp(jnp.sum(lq2_ref[...] * lk2_ref[...], axis=-1, keepdims=True)) + LAM_INIT)
        acc = acc_sc[...]
        o = acc[:HEAD_DIM] * (1.0 / acc[HEAD_DIM:HEAD_DIM + 1])
        od = o[:, :tq] - lam * o[:, tq:]
        ms = jnp.mean(od * od, axis=0, keepdims=True)
        on = od * lax.rsqrt(ms + EPS) * sg_ref[...] * (1.0 - LAM_INIT)
        o_ref[...] = on.T.astype(BF16)

    for n in range(SEQ // tq):
        pl.when(i == n)(functools.partial(run, n))


def _attention(q, k, vt, km, vmt, lq1, lk1, lq2, lk2, sg, cast_ws, *, batch, tq, cw):
    tk = tq // 2
    assert tk % cw == 0 and cw % CHUNK == 0
    nq = SEQ // tq
    steps = batch * N_HEADS * nq
    lspec = pl.BlockSpec((1, QK_DIM), lambda b, h, i: (0, 0))

    def slab(w):
        rep = next(r for r in (1, 2, 4, 8) if w.shape[0] * r % (BF16_ROWS * steps) == 0)
        return pl.BlockSpec((w.shape[0] * rep // steps, w.shape[1]),
                            lambda b, h, i: (((b * N_HEADS + h) * nq + i) // rep, 0))

    outs = pl.pallas_call(
        functools.partial(_attn_kernel, tq=tq, tk=tk, cw=cw, n_cast=len(cast_ws)),
        grid=(batch, N_HEADS, nq),
        in_specs=[
            pl.BlockSpec((tq, HEAD_DIM), lambda b, h, i: (b * nq + i, h)),
            pl.BlockSpec((SEQ, HEAD_DIM), lambda b, h, i: (b, h)),
            pl.BlockSpec((HEAD_DIM, SEQ), lambda b, h, i: (h, b)),
            pl.BlockSpec((N_META, HEAD_DIM), lambda b, h, i: (0, h)),
            pl.BlockSpec((HEAD_DIM, N_META), lambda b, h, i: (h, 0)),
            lspec, lspec, lspec, lspec,
            pl.BlockSpec((HEAD_DIM, 1), lambda b, h, i: (0, 0)),
        ] + [slab(w) for w in cast_ws],
        out_specs=[pl.BlockSpec((tq, HEAD_DIM), lambda b, h, i: (b * nq + i, h))] + [slab(w) for w in cast_ws],
        out_shape=[jax.ShapeDtypeStruct((batch * SEQ, ATTN_WIDTH), BF16)]
        + [jax.ShapeDtypeStruct(w.shape, BF16) for w in cast_ws],
        scratch_shapes=[
            pltpu.VMEM((1, 2 * tq), F32),
            pltpu.VMEM((HEAD_DIM + DENOM_ROWS, 2 * tq), F32),
            pltpu.VMEM((tk, 2 * tq), F32),
            pltpu.VMEM((tk, 2 * tq), F32),
        ],
        compiler_params=_params("parallel", "parallel", "arbitrary"),
        name="diff_attn",
    )(q, k, vt, km, vmt, lq1, lk1, lq2, lk2, sg, *cast_ws)
    return outs[0], outs[1:]


def _conv_kernel(u_ref, prev_ref, halo0_ref, w_ref, b_ref, g_ref, o_ref, win_sc, shift_sc, y_sc, *, tl, rc, rb):
    i = pl.program_id(1)

    @pl.when(i == 0)
    def _():
        win_sc[0:HALO, :] = halo0_ref[...]

    @pl.when(i > 0)
    def _():
        win_sc[0:HALO, :] = prev_ref[...]

    win_sc[HALO:HALO + tl, :] = u_ref[...]

    nsh = HALO + tl - SUBLANES
    for s in range(1, SUBLANES):
        shift_sc[s - 1] = win_sc[s:s + nsh, :]

    first = HALO - (CONV_KERNEL - 1)
    @pl.loop(0, CONV_WIDTH // LANES)
    def _(c):
        cs = pl.ds(pl.multiple_of(c * LANES, LANES), LANES)
        for r0 in range(0, tl // rc, rb):
            accs = [jnp.broadcast_to(b_ref[:, cs], (rc, LANES))] * rb
            for t in range(CONV_KERNEL):
                wt = w_ref[t:t + 1, cs]
                for k in range(rb):
                    lo = (r0 + k) * rc + first + t
                    s, al = lo % SUBLANES, lo - lo % SUBLANES
                    tap = win_sc[al:al + rc, cs] if s == 0 else shift_sc[s - 1, al:al + rc, cs]
                    accs[k] = accs[k] + wt * tap
            for k in range(rb):
                y_sc[(r0 + k) * rc:(r0 + k + 1) * rc, cs] = accs[k]

    yn = _rms(y_sc[...], g_ref[...])
    o_ref[...] = (yn * _sigmoid(yn)).astype(BF16)


def _conv(u, halo0, w, b, g, *, batch, tl, rc, rb):
    nt = SEQ // tl
    hb = tl // HALO
    return pl.pallas_call(
        functools.partial(_conv_kernel, tl=tl, rc=rc, rb=rb),
        grid=(batch, nt),
        in_specs=[
            pl.BlockSpec((tl, CONV_WIDTH), lambda bb, i: (bb * nt + i, 0)),
            pl.BlockSpec((HALO, CONV_WIDTH), lambda bb, i: (jnp.maximum((bb * nt + i) * hb - 1, 0), 0)),
            pl.BlockSpec((HALO, CONV_WIDTH), lambda bb, i: (0, 0)),
            pl.BlockSpec((CONV_KERNEL, CONV_WIDTH), lambda bb, i: (0, 0)),
            pl.BlockSpec((1, CONV_WIDTH), lambda bb, i: (0, 0)),
            pl.BlockSpec((1, CONV_WIDTH), lambda bb, i: (0, 0)),
        ],
        out_specs=pl.BlockSpec((tl, CONV_WIDTH), lambda bb, i: (bb * nt + i, 0)),
        out_shape=jax.ShapeDtypeStruct((batch * SEQ, CONV_WIDTH), BF16),
        scratch_shapes=[
            pltpu.VMEM((HALO + tl, CONV_WIDTH), F32),
            pltpu.VMEM((SUBLANES - 1, HALO + tl - SUBLANES, CONV_WIDTH), F32),
            pltpu.VMEM((tl, CONV_WIDTH), F32),
        ],
        compiler_params=_params("parallel", "arbitrary"),
        name="conv",
    )(u, u, halo0, w, b, g)


def _outproj_kernel(x_ref, a_ref, c_ref, wa_ref, wc_ref, o_ref):
    o_ref[...] = (x_ref[...]
                  + jnp.dot(a_ref[...], wa_ref[...].astype(BF16), preferred_element_type=F32)
                  + jnp.dot(c_ref[...], wc_ref[...].astype(BF16), preferred_element_type=F32))


def _out_proj(x, oa, oc, w_out, *, tm, tn):
    m = x.shape[0]
    return pl.pallas_call(
        _outproj_kernel,
        grid=(m // tm, D_MODEL // tn),
        in_specs=[
            pl.BlockSpec((tm, tn), lambda i, j: (i, j)),
            pl.BlockSpec((tm, ATTN_WIDTH), lambda i, j: (i, 0)),
            pl.BlockSpec((tm, CONV_WIDTH), lambda i, j: (i, 0)),
            pl.BlockSpec((ATTN_WIDTH, tn), lambda i, j: (0, j)),
            pl.BlockSpec((CONV_WIDTH, tn), lambda i, j: (1, j)),
        ],
        out_specs=pl.BlockSpec((tm, tn), lambda i, j: (i, j)),
        out_shape=jax.ShapeDtypeStruct((m, D_MODEL), F32),
        compiler_params=_params("parallel", "arbitrary"),
        name="out_proj",
    )(x, oa, oc, w_out, w_out)


def kernel(x, meta_tokens, ffn1_norm_g, ffn1_w_gate, ffn1_w_up, ffn1_w_down, mix_norm_g, w_in, q_norm_g,
           k_norm_g, lambda_q1, lambda_k1, lambda_q2, lambda_k2, attn_subln_g, conv_w, conv_b, conv_norm_g,
           w_out, ffn2_norm_g, ffn2_w_gate, ffn2_w_up, ffn2_w_down, final_norm_g):
    batch = x.shape[0]
    xr = x.reshape(batch * SEQ, D_MODEL)

    w1g, w1u, w1d = ffn1_w_gate[0], ffn1_w_up[0], ffn1_w_down[0]
    w2g, w2u, w2d = ffn2_w_gate[0], ffn2_w_up[0], ffn2_w_down[0]
    win, wout = w_in[0], w_out[0]
    qg = jnp.tile(q_norm_g[0] * (QK_DIM ** -0.5 * LOG2E), 2)[None, :]
    kg = jnp.tile(k_norm_g[0], 2)[None, :]
    sg = attn_subln_g[0][:, None]
    cw = conv_w[0].reshape(CONV_KERNEL, CONV_WIDTH)

    m1, (w1g, w1u, w1d) = _ffn(meta_tokens, ffn1_norm_g, w1g, w1u, w1d, final_norm_g,
                               tm=N_META, tf=512, final_norm=False, emit_w16=True)
    (_, km, vmt, um), win16 = _in_proj(m1, mix_norm_g, win, qg, kg, tm=N_META, tn=256, emit_w16=True)
    halo0 = jnp.concatenate([jnp.zeros((HALO - N_META, CONV_WIDTH), F32), um], axis=0)

    x1 = _ffn(xr, ffn1_norm_g, w1g, w1u, w1d, final_norm_g, tm=1024, tf=512, final_norm=False)
    q, k, vt, u = _in_proj(x1, mix_norm_g, win16, qg, kg, tm=1024, tn=256)
    oa, (w2g, w2u, w2d, wout) = _attention(q, k, vt, km, vmt, lambda_q1, lambda_k1, lambda_q2, lambda_k2, sg,
                                           [w2g, w2u, w2d, wout], batch=batch, tq=1024, cw=256)
    oc = _conv(u, halo0, cw, conv_b, conv_norm_g, batch=batch, tl=256, rc=32, rb=8)
    x2 = _out_proj(x1, oa, oc, wout, tm=512, tn=D_MODEL)
    y = _ffn(x2, ffn2_norm_g, w2g, w2u, w2d, final_norm_g, tm=1024, tf=512, final_norm=True)
    return y.reshape(batch, SEQ, D_MODEL)
```

```python
import functools
import math

import jax
import jax.numpy as jnp
from jax import lax
from jax.experimental import pallas as pl
from jax.experimental.pallas import tpu as pltpu

F32 = jnp.float32
BF16 = jnp.bfloat16

D_MODEL = 2048
SEQ = 4096
N_META = 16
CHUNK = 64
ATTN_WIDTH = 1024
CONV_WIDTH = 1024
HEAD_DIM = 128
N_HEADS = ATTN_WIDTH // HEAD_DIM
QK_DIM = HEAD_DIM // 2
CONV_KERNEL = 31
D_FF = 5632
EPS = 1e-6
NEG_INF = -1e30
LAM_INIT = 0.8 - 0.6 * math.exp(-0.3 * 1)

LOG2E = math.log2(math.e)

LANES = 128
SUBLANES = 8
BF16_ROWS = 16
DENOM_ROWS = 16
HALO = 32
VMEM_LIMIT = 56 * 1024 * 1024


def _params(*sem):
    return pltpu.CompilerParams(dimension_semantics=sem, vmem_limit_bytes=VMEM_LIMIT)


def _rms(x, g):
    ms = jnp.mean(x * x, axis=-1, keepdims=True)
    return x * lax.rsqrt(ms + EPS) * g


def _sigmoid(x):
    return 1.0 / (1.0 + jnp.exp(-x))


def _ffn_kernel(x_ref, g_ref, wg_ref, wu_ref, wd_ref, fg_ref, o_ref, *rest, final_norm, emit_w16):
    h_ref = rest[-1]
    j = pl.program_id(1)

    @pl.when(j == 0)
    def _():
        x = x_ref[...]
        h_ref[...] = _rms(x, g_ref[...]).astype(BF16)
        o_ref[...] = x

    wg, wu, wd = (w_ref[...].astype(BF16) for w_ref in (wg_ref, wu_ref, wd_ref))
    if emit_w16:
        for w16_ref, w in zip(rest[:3], (wg, wu, wd)):
            w16_ref[...] = w
    h = h_ref[...]
    gate = jnp.dot(h, wg, preferred_element_type=F32)
    up = jnp.dot(h, wu, preferred_element_type=F32)
    a = (gate * _sigmoid(gate)) * (up * 0.5)
    o_ref[...] += jnp.dot(a.astype(BF16), wd, preferred_element_type=F32)

    if final_norm:
        @pl.when(j == pl.num_programs(1) - 1)
        def _():
            o_ref[...] = _rms(o_ref[...], fg_ref[...])


def _ffn(x, g, wg, wu, wd, fg, *, tm, tf, final_norm, emit_w16=False):
    m = x.shape[0]
    grid = (m // tm, D_FF // tf)
    assert not emit_w16 or grid[0] == 1
    wspecs = [
        pl.BlockSpec((D_MODEL, tf), lambda i, j: (0, j)),
        pl.BlockSpec((D_MODEL, tf), lambda i, j: (0, j)),
        pl.BlockSpec((tf, D_MODEL), lambda i, j: (j, 0)),
    ]
    out_specs = [pl.BlockSpec((tm, D_MODEL), lambda i, j: (i, 0))]
    out_shape = [jax.ShapeDtypeStruct((m, D_MODEL), F32)]
    if emit_w16:
        out_specs += wspecs
        out_shape += [jax.ShapeDtypeStruct(w.shape, BF16) for w in (wg, wu, wd)]
    outs = pl.pallas_call(
        functools.partial(_ffn_kernel, final_norm=final_norm, emit_w16=emit_w16),
        grid=grid,
        in_specs=[
            pl.BlockSpec((tm, D_MODEL), lambda i, j: (i, 0)),
            pl.BlockSpec((1, D_MODEL), lambda i, j: (0, 0)),
            *wspecs,
            pl.BlockSpec((1, D_MODEL), lambda i, j: (0, 0)),
        ],
        out_specs=out_specs,
        out_shape=out_shape,
        scratch_shapes=[pltpu.VMEM((tm, D_MODEL), BF16)],
        compiler_params=_params("parallel", "arbitrary"),
        name="ffn_final" if final_norm else "ffn",
    )(x, g, wg, wu, wd, fg)
    return (outs[0], outs[1:]) if emit_w16 else outs[0]


def _qk_norm(x, g):
    outs = []
    for hh in range(x.shape[1] // LANES):
        xh = x[:, hh * LANES:(hh + 1) * LANES]
        lo = lax.broadcasted_iota(jnp.int32, xh.shape, 1) < QK_DIM
        x2 = xh * xh
        s_lo = jnp.sum(jnp.where(lo, x2, 0.0), axis=-1, keepdims=True)
        s_hi = jnp.sum(jnp.where(lo, 0.0, x2), axis=-1, keepdims=True)
        ms = jnp.where(lo, s_lo, s_hi) * (1.0 / QK_DIM)
        outs.append(xh * lax.rsqrt(ms + EPS) * g)
    return outs[0] if len(outs) == 1 else jnp.concatenate(outs, axis=1)


def _inproj_kernel(x_ref, g_ref, wq_ref, wk_ref, wv_ref, wa_ref, wc_ref, qg_ref, kg_ref,
                   q_ref, k_ref, vt_ref, u_ref, *rest, emit_w16):
    h_ref = rest[-1]
    j = pl.program_id(1)

    @pl.when(j == 0)
    def _():
        h_ref[...] = _rms(x_ref[...], g_ref[...]).astype(BF16)

    ws = [w_ref[...].astype(BF16) for w_ref in (wq_ref, wk_ref, wv_ref, wa_ref, wc_ref)]
    if emit_w16:
        for w16_ref, w in zip(rest[:5], ws):
            w16_ref[...] = w
    h = h_ref[...]
    q, k, v, ca, cg = (jnp.dot(h, w, preferred_element_type=F32) for w in ws)
    q_ref[...] = _qk_norm(q, qg_ref[...]).astype(BF16)
    k_ref[...] = _qk_norm(k, kg_ref[...]).astype(BF16)
    vt_ref[...] = v.T.astype(BF16)
    u_ref[...] = ca * _sigmoid(cg)


def _in_proj(x, g, ws, qg, kg, *, tm, tn, emit_w16=False):
    m = x.shape[0]
    nb = ATTN_WIDTH // tn
    assert not emit_w16 or m == tm
    if isinstance(ws, (tuple, list)):
        wspecs = [pl.BlockSpec((D_MODEL, tn), lambda i, j: (0, j))] * 5
    else:
        wspecs = [pl.BlockSpec((D_MODEL, tn), lambda i, j, c=c: (0, c * nb + j)) for c in range(5)]
        ws = [ws] * 5
    out_specs = [
        pl.BlockSpec((tm, tn), lambda i, j: (i, j)),
        pl.BlockSpec((tm, tn), lambda i, j: (i, j)),
        pl.BlockSpec((tn, tm), lambda i, j: (j, i)),
        pl.BlockSpec((tm, tn), lambda i, j: (i, j)),
    ]
    out_shape = [
        jax.ShapeDtypeStruct((m, ATTN_WIDTH), BF16),
        jax.ShapeDtypeStruct((m, ATTN_WIDTH), BF16),
        jax.ShapeDtypeStruct((ATTN_WIDTH, m), BF16),
        jax.ShapeDtypeStruct((m, CONV_WIDTH), F32),
    ]
    if emit_w16:
        out_specs += [pl.BlockSpec((D_MODEL, tn), lambda i, j: (0, j))] * 5
        out_shape += [jax.ShapeDtypeStruct((D_MODEL, ATTN_WIDTH), BF16)] * 5
    outs = pl.pallas_call(
        functools.partial(_inproj_kernel, emit_w16=emit_w16),
        grid=(m // tm, nb),
        in_specs=[
            pl.BlockSpec((tm, D_MODEL), lambda i, j: (i, 0)),
            pl.BlockSpec((1, D_MODEL), lambda i, j: (0, 0)),
            *wspecs,
            pl.BlockSpec((1, LANES), lambda i, j: (0, 0)),
            pl.BlockSpec((1, LANES), lambda i, j: (0, 0)),
        ],
        out_specs=out_specs,
        out_shape=out_shape,
        scratch_shapes=[pltpu.VMEM((tm, D_MODEL), BF16)],
        compiler_params=_params("parallel", "arbitrary"),
        name="in_proj",
    )(x, g, *ws, qg, kg)
    return (outs[:4], outs[4:]) if emit_w16 else outs


def _attn_kernel(q_ref, k_ref, vt_ref, km_ref, vmt_ref, lq1_ref, lk1_ref, lq2_ref, lk2_ref, sg_ref,
                 *rest, tq, tk, cw, n_cast):
    w32_refs, o_ref, w16_refs = rest[:n_cast], rest[n_cast], rest[n_cast + 1:2 * n_cast + 1]
    m_sc, acc_sc, s0_sc, s1_sc = rest[2 * n_cast + 1:]
    i = pl.program_id(2)
    qt = q_ref[...].astype(F32).T
    row = lax.broadcasted_iota(jnp.int32, qt.shape, 0)
    qs = jnp.concatenate([jnp.where(row < QK_DIM, qt, 0.0),
                          jnp.where(row < QK_DIM, 0.0, qt)], axis=1).astype(BF16)

    def scores(blk, s_ref):
        s_ref[...] = jnp.dot(k_ref[blk * tk:(blk + 1) * tk, :], qs, preferred_element_type=F32)

    def with_ones(vt):
        return jnp.concatenate([vt, jnp.ones((DENOM_ROWS, vt.shape[1]), BF16)], axis=0)

    def online_update(s, vt, m_old, acc_old):
        m_new = jnp.maximum(m_old, jnp.max(s, axis=0, keepdims=True))
        alpha = jnp.exp2(m_old - m_new)
        p = jnp.exp2(s - m_new)
        return m_new, alpha * acc_old + jnp.dot(with_ones(vt), p.astype(BF16), preferred_element_type=F32)

    def softmax_pv(blk, s_ref, diag):
        off = blk * tk
        for c in range(2 * tq // cw):
            cs = slice(c * cw, (c + 1) * cw)
            nk, masked = tk, False
            if diag is not None:
                q0 = (c * cw) % tq
                nk = min(tk, q0 + cw - diag * tk)
                if nk <= 0:
                    continue
                masked = (diag * tk + nk - 1) // CHUNK > q0 // CHUNK

            def load_scores():
                s = s_ref[0:nk, cs]
                if masked:
                    kchunk = (lax.broadcasted_iota(jnp.int32, (nk, cw), 0) + diag * tk) // CHUNK
                    qchunk = (lax.broadcasted_iota(jnp.int32, (nk, cw), 1) + q0) // CHUNK
                    s = jnp.where(kchunk <= qchunk, s, NEG_INF)
                return s

            m_old = m_sc[:, cs]
            m_new = jnp.maximum(m_old, jnp.max(load_scores(), axis=0, keepdims=True))
            m_sc[:, cs] = m_new
            p = jnp.exp2(load_scores() - m_new)
            pv = jnp.dot(with_ones(vt_ref[:, off:off + nk]), p.astype(BF16), preferred_element_type=F32)
            acc_sc[:, cs] = jnp.exp2(m_old - m_new) * acc_sc[:, cs] + pv

    def run(n_pairs):
        for w32, w16 in zip(w32_refs, w16_refs):
            w16[...] = w32[...].astype(BF16)
        scores(0, s0_sc)
        s = jnp.dot(km_ref[...], qs, preferred_element_type=F32)
        m_sc[...], acc_sc[...] = online_update(s, vmt_ref[...], jnp.full(m_sc.shape, NEG_INF, F32),
                                               jnp.zeros(acc_sc.shape, F32))
        for jp in range(n_pairs):
            scores(2 * jp + 1, s1_sc)
            softmax_pv(2 * jp, s0_sc, None)
            scores(2 * jp + 2, s0_sc)
            softmax_pv(2 * jp + 1, s1_sc, None)

        scores(2 * n_pairs + 1, s1_sc)
        softmax_pv(2 * n_pairs, s0_sc, 0)
        softmax_pv(2 * n_pairs + 1, s1_sc, 1)

        lam = (jnp.exp(jnp.sum(lq1_ref[...] * lk1_ref[...], axis=-1, keepdims=True))
               - jnp.exp(jnp.sum(lq2_ref[...] * lk2_ref[...], axis=-1, keepdims=True)) + LAM_INIT)
        acc = acc_sc[...]
        o = acc[:HEAD_DIM] * (1.0 / acc[HEAD_DIM:HEAD_DIM + 1])
        od = o[:, :tq] - lam * o[:, tq:]
        ms = jnp.mean(od * od, axis=0, keepdims=True)
        on = od * lax.rsqrt(ms + EPS) * sg_ref[...] * (1.0 - LAM_INIT)
        o_ref[...] = on.T.astype(BF16)

    for n in range(SEQ // tq):
        pl.when(i == n)(functools.partial(run, n))


def _attention(q, k, vt, km, vmt, lq1, lk1, lq2, lk2, sg, cast_ws, *, batch, tq, cw):
    tk = tq // 2
    assert tk % cw == 0 and cw % CHUNK == 0
    nq = SEQ // tq
    steps = batch * N_HEADS * nq
    lspec = pl.BlockSpec((1, QK_DIM), lambda b, h, i: (0, 0))

    def slab(w):
        rep = next(r for r in (1, 2, 4, 8) if w.shape[0] * r % (BF16_ROWS * steps) == 0)
        return pl.BlockSpec((w.shape[0] * rep // steps, w.shape[1]),
                            lambda b, h, i: (((b * N_HEADS + h) * nq + i) // rep, 0))

    outs = pl.pallas_call(
        functools.partial(_attn_kernel, tq=tq, tk=tk, cw=cw, n_cast=len(cast_ws)),
        grid=(batch, N_HEADS, nq),
        in_specs=[
            pl.BlockSpec((tq, HEAD_DIM), lambda b, h, i: (b * nq + i, h)),
            pl.BlockSpec((SEQ, HEAD_DIM), lambda b, h, i: (b, h)),
            pl.BlockSpec((HEAD_DIM, SEQ), lambda b, h, i: (h, b)),
            pl.BlockSpec((N_META, HEAD_DIM), lambda b, h, i: (0, h)),
            pl.BlockSpec((HEAD_DIM, N_META), lambda b, h, i: (h, 0)),
            lspec, lspec, lspec, lspec,
            pl.BlockSpec((HEAD_DIM, 1), lambda b, h, i: (0, 0)),
        ] + [slab(w) for w in cast_ws],
        out_specs=[pl.BlockSpec((tq, HEAD_DIM), lambda b, h, i: (b * nq + i, h))] + [slab(w) for w in cast_ws],
        out_shape=[jax.ShapeDtypeStruct((batch * SEQ, ATTN_WIDTH), BF16)]
        + [jax.ShapeDtypeStruct(w.shape, BF16) for w in cast_ws],
        scratch_shapes=[
            pltpu.VMEM((1, 2 * tq), F32),
            pltpu.VMEM((HEAD_DIM + DENOM_ROWS, 2 * tq), F32),
            pltpu.VMEM((tk, 2 * tq), F32),
            pltpu.VMEM((tk, 2 * tq), F32),
        ],
        compiler_params=_params("parallel", "parallel", "arbitrary"),
        name="diff_attn",
    )(q, k, vt, km, vmt, lq1, lk1, lq2, lk2, sg, *cast_ws)
    return outs[0], outs[1:]


def _conv_kernel(u_ref, prev_ref, halo0_ref, w_ref, b_ref, g_ref, o_ref, win_sc, shift_sc, y_sc, *, tl, rc, rb):
    i = pl.program_id(1)

    @pl.when(i == 0)
    def _():
        win_sc[0:HALO, :] = halo0_ref[...]

    @pl.when(i > 0)
    def _():
        win_sc[0:HALO, :] = prev_ref[...]

    win_sc[HALO:HALO + tl, :] = u_ref[...]

    nsh = HALO + tl - SUBLANES
    for s in range(1, SUBLANES):
        shift_sc[s - 1] = win_sc[s:s + nsh, :]

    first = HALO - (CONV_KERNEL - 1)
    @pl.loop(0, CONV_WIDTH // LANES)
    def _(c):
        cs = pl.ds(pl.multiple_of(c * LANES, LANES), LANES)
        for r0 in range(0, tl // rc, rb):
            accs = [jnp.broadcast_to(b_ref[:, cs], (rc, LANES))] * rb
            for t in range(CONV_KERNEL):
                wt = w_ref[t:t + 1, cs]
                for k in range(rb):
                    lo = (r0 + k) * rc + first + t
                    s, al = lo % SUBLANES, lo - lo % SUBLANES
                    tap = win_sc[al:al + rc, cs] if s == 0 else shift_sc[s - 1, al:al + rc, cs]
                    accs[k] = accs[k] + wt * tap
            for k in range(rb):
                y_sc[(r0 + k) * rc:(r0 + k + 1) * rc, cs] = accs[k]

    yn = _rms(y_sc[...], g_ref[...])
    o_ref[...] = (yn * _sigmoid(yn)).astype(BF16)


def _conv(u, halo0, w, b, g, *, batch, tl, rc, rb):
    nt = SEQ // tl
    hb = tl // HALO
    return pl.pallas_call(
        functools.partial(_conv_kernel, tl=tl, rc=rc, rb=rb),
        grid=(batch, nt),
        in_specs=[
            pl.BlockSpec((tl, CONV_WIDTH), lambda bb, i: (bb * nt + i, 0)),
            pl.BlockSpec((HALO, CONV_WIDTH), lambda bb, i: (jnp.maximum((bb * nt + i) * hb - 1, 0), 0)),
            pl.BlockSpec((HALO, CONV_WIDTH), lambda bb, i: (0, 0)),
            pl.BlockSpec((CONV_KERNEL, CONV_WIDTH), lambda bb, i: (0, 0)),
            pl.BlockSpec((1, CONV_WIDTH), lambda bb, i: (0, 0)),
            pl.BlockSpec((1, CONV_WIDTH), lambda bb, i: (0, 0)),
        ],
        out_specs=pl.BlockSpec((tl, CONV_WIDTH), lambda bb, i: (bb * nt + i, 0)),
        out_shape=jax.ShapeDtypeStruct((batch * SEQ, CONV_WIDTH), BF16),
        scratch_shapes=[
            pltpu.VMEM((HALO + tl, CONV_WIDTH), F32),
            pltpu.VMEM((SUBLANES - 1, HALO + tl - SUBLANES, CONV_WIDTH), F32),
            pltpu.VMEM((tl, CONV_WIDTH), F32),
        ],
        compiler_params=_params("parallel", "arbitrary"),
        name="conv",
    )(u, u, halo0, w, b, g)


def _outproj_kernel(x_ref, a_ref, c_ref, wa_ref, wc_ref, o_ref):
    o_ref[...] = (x_ref[...]
                  + jnp.dot(a_ref[...], wa_ref[...].astype(BF16), preferred_element_type=F32)
                  + jnp.dot(c_ref[...], wc_ref[...].astype(BF16), preferred_element_type=F32))


def _out_proj(x, oa, oc, w_out, *, tm, tn):
    m = x.shape[0]
    return pl.pallas_call(
        _outproj_kernel,
        grid=(m // tm, D_MODEL // tn),
        in_specs=[
            pl.BlockSpec((tm, tn), lambda i, j: (i, j)),
            pl.BlockSpec((tm, ATTN_WIDTH), lambda i, j: (i, 0)),
            pl.BlockSpec((tm, CONV_WIDTH), lambda i, j: (i, 0)),
            pl.BlockSpec((ATTN_WIDTH, tn), lambda i, j: (0, j)),
            pl.BlockSpec((CONV_WIDTH, tn), lambda i, j: (1, j)),
        ],
        out_specs=pl.BlockSpec((tm, tn), lambda i, j: (i, j)),
        out_shape=jax.ShapeDtypeStruct((m, D_MODEL), F32),
        compiler_params=_params("parallel", "arbitrary"),
        name="out_proj",
    )(x, oa, oc, w_out, w_out)


def kernel(x, meta_tokens, ffn1_norm_g, ffn1_w_gate, ffn1_w_up, ffn1_w_down, mix_norm_g, w_in, q_norm_g,
           k_norm_g, lambda_q1, lambda_k1, lambda_q2, lambda_k2, attn_subln_g, conv_w, conv_b, conv_norm_g,
           w_out, ffn2_norm_g, ffn2_w_gate, ffn2_w_up, ffn2_w_down, final_norm_g):
    batch = x.shape[0]
    xr = x.reshape(batch * SEQ, D_MODEL)

    w1g, w1u, w1d = ffn1_w_gate[0], ffn1_w_up[0], ffn1_w_down[0]
    w2g, w2u, w2d = ffn2_w_gate[0], ffn2_w_up[0], ffn2_w_down[0]
    win, wout = w_in[0], w_out[0]
    qg = jnp.tile(q_norm_g[0] * (QK_DIM ** -0.5 * LOG2E), 2)[None, :]
    kg = jnp.tile(k_norm_g[0], 2)[None, :]
    sg = attn_subln_g[0][:, None]
    cw = conv_w[0].reshape(CONV_KERNEL, CONV_WIDTH)

    m1, (w1g, w1u, w1d) = _ffn(meta_tokens, ffn1_norm_g, w1g, w1u, w1d, final_norm_g,
                               tm=N_META, tf=512, final_norm=False, emit_w16=True)
    (_, km, vmt, um), win16 = _in_proj(m1, mix_norm_g, win, qg, kg, tm=N_META, tn=256, emit_w16=True)
    halo0 = jnp.concatenate([jnp.zeros((HALO - N_META, CONV_WIDTH), F32), um], axis=0)

    x1 = _ffn(xr, ffn1_norm_g, w1g, w1u, w1d, final_norm_g, tm=1024, tf=512, final_norm=False)
    q, k, vt, u = _in_proj(x1, mix_norm_g, win16, qg, kg, tm=1024, tn=256)
    oa, (w2g, w2u, w2d, wout) = _attention(q, k, vt, km, vmt, lambda_q1, lambda_k1, lambda_q2, lambda_k2, sg,
                                           [w2g, w2u, w2d, wout], batch=batch, tq=1024, cw=256)
    oc = _conv(u, halo0, cw, conv_b, conv_norm_g, batch=batch, tl=256, rc=32, rb=8)
    x2 = _out_proj(x1, oa, oc, wout, tm=512, tn=D_MODEL)
    y = _ffn(x2, ffn2_norm_g, w2g, w2u, w2d, final_norm_g, tm=1024, tf=512, final_norm=True)
    return y.reshape(batch, SEQ, D_MODEL)
```

```python
import functools
import math

import jax
import jax.numpy as jnp
from jax import lax
from jax.experimental import pallas as pl
from jax.experimental.pallas import tpu as pltpu

F32 = jnp.float32
BF16 = jnp.bfloat16

D_MODEL = 2048
SEQ = 4096
N_META = 16
CHUNK = 64
ATTN_WIDTH = 1024
CONV_WIDTH = 1024
HEAD_DIM = 128
N_HEADS = ATTN_WIDTH // HEAD_DIM
QK_DIM = HEAD_DIM // 2
CONV_KERNEL = 31
D_FF = 5632
EPS = 1e-6
NEG_INF = -1e30
LAM_INIT = 0.8 - 0.6 * math.exp(-0.3 * 1)

LOG2E = math.log2(math.e)

LANES = 128
SUBLANES = 8
BF16_ROWS = 16
DENOM_ROWS = 16
HALO = 32
VMEM_LIMIT = 56 * 1024 * 1024


def _params(*sem):
    return pltpu.CompilerParams(dimension_semantics=sem, vmem_limit_bytes=VMEM_LIMIT)


def _rms(x, g):
    ms = jnp.mean(x * x, axis=-1, keepdims=True)
    return x * lax.rsqrt(ms + EPS) * g


def _sigmoid(x):
    return 1.0 / (1.0 + jnp.exp(-x))


def _ffn_kernel(x_ref, g_ref, wg_ref, wu_ref, wd_ref, fg_ref, o_ref, *rest, final_norm, emit_w16):
    h_ref = rest[-1]
    j = pl.program_id(1)

    @pl.when(j == 0)
    def _():
        x = x_ref[...]
        h_ref[...] = _rms(x, g_ref[...]).astype(BF16)
        o_ref[...] = x

    wg, wu, wd = (w_ref[...].astype(BF16) for w_ref in (wg_ref, wu_ref, wd_ref))
    if emit_w16:
        for w16_ref, w in zip(rest[:3], (wg, wu, wd)):
            w16_ref[...] = w
    h = h_ref[...]
    gate = jnp.dot(h, wg, preferred_element_type=F32)
    up = jnp.dot(h, wu, preferred_element_type=F32)
    a = (gate * _sigmoid(gate)) * (up * 0.5)
    o_ref[...] += jnp.dot(a.astype(BF16), wd, preferred_element_type=F32)

    if final_norm:
        @pl.when(j == pl.num_programs(1) - 1)
        def _():
            o_ref[...] = _rms(o_ref[...], fg_ref[...])


def _ffn(x, g, wg, wu, wd, fg, *, tm, tf, final_norm, emit_w16=False):
    m = x.shape[0]
    grid = (m // tm, D_FF // tf)
    assert not emit_w16 or grid[0] == 1
    wspecs = [
        pl.BlockSpec((D_MODEL, tf), lambda i, j: (0, j)),
        pl.BlockSpec((D_MODEL, tf), lambda i, j: (0, j)),
        pl.BlockSpec((tf, D_MODEL), lambda i, j: (j, 0)),
    ]
    out_specs = [pl.BlockSpec((tm, D_MODEL), lambda i, j: (i, 0))]
    out_shape = [jax.ShapeDtypeStruct((m, D_MODEL), F32)]
    if emit_w16:
        out_specs += wspecs
        out_shape += [jax.ShapeDtypeStruct(w.shape, BF16) for w in (wg, wu, wd)]
    outs = pl.pallas_call(
        functools.partial(_ffn_kernel, final_norm=final_norm, emit_w16=emit_w16),
        grid=grid,
        in_specs=[
            pl.BlockSpec((tm, D_MODEL), lambda i, j: (i, 0)),
            pl.BlockSpec((1, D_MODEL), lambda i, j: (0, 0)),
            *wspecs,
            pl.BlockSpec((1, D_MODEL), lambda i, j: (0, 0)),
        ],
        out_specs=out_specs,
        out_shape=out_shape,
        scratch_shapes=[pltpu.VMEM((tm, D_MODEL), BF16)],
        compiler_params=_params("parallel", "arbitrary"),
        name="ffn_final" if final_norm else "ffn",
    )(x, g, wg, wu, wd, fg)
    return (outs[0], outs[1:]) if emit_w16 else outs[0]


def _qk_norm(x, g):
    outs = []
    for hh in range(x.shape[1] // LANES):
        xh = x[:, hh * LANES:(hh + 1) * LANES]
        lo = lax.broadcasted_iota(jnp.int32, xh.shape, 1) < QK_DIM
        x2 = xh * xh
        s_lo = jnp.sum(jnp.where(lo, x2, 0.0), axis=-1, keepdims=True)
        s_hi = jnp.sum(jnp.where(lo, 0.0, x2), axis=-1, keepdims=True)
        ms = jnp.where(lo, s_lo, s_hi) * (1.0 / QK_DIM)
        outs.append(xh * lax.rsqrt(ms + EPS) * g)
    return outs[0] if len(outs) == 1 else jnp.concatenate(outs, axis=1)


def _inproj_kernel(x_ref, g_ref, wq_ref, wk_ref, wv_ref, wa_ref, wc_ref, qg_ref, kg_ref,
                   q_ref, k_ref, vt_ref, u_ref, *rest, emit_w16):
    h_ref = rest[-1]
    j = pl.program_id(1)

    @pl.when(j == 0)
    def _():
        h_ref[...] = _rms(x_ref[...], g_ref[...]).astype(BF16)

    ws = [w_ref[...].astype(BF16) for w_ref in (wq_ref, wk_ref, wv_ref, wa_ref, wc_ref)]
    if emit_w16:
        for w16_ref, w in zip(rest[:5], ws):
            w16_ref[...] = w
    h = h_ref[...]
    q, k, v, ca, cg = (jnp.dot(h, w, preferred_element_type=F32) for w in ws)
    q_ref[...] = _qk_norm(q, qg_ref[...]).astype(BF16)
    k_ref[...] = _qk_norm(k, kg_ref[...]).astype(BF16)
    vt_ref[...] = v.T.astype(BF16)
    u_ref[...] = ca * _sigmoid(cg)


def _in_proj(x, g, ws, qg, kg, *, tm, tn, emit_w16=False):
    m = x.shape[0]
    nb = ATTN_WIDTH // tn
    assert not emit_w16 or m == tm
    if isinstance(ws, (tuple, list)):
        wspecs = [pl.BlockSpec((D_MODEL, tn), lambda i, j: (0, j))] * 5
    else:
        wspecs = [pl.BlockSpec((D_MODEL, tn), lambda i, j, c=c: (0, c * nb + j)) for c in range(5)]
        ws = [ws] * 5
    out_specs = [
        pl.BlockSpec((tm, tn), lambda i, j: (i, j)),
        pl.BlockSpec((tm, tn), lambda i, j: (i, j)),
        pl.BlockSpec((tn, tm), lambda i, j: (j, i)),
        pl.BlockSpec((tm, tn), lambda i, j: (i, j)),
    ]
    out_shape = [
        jax.ShapeDtypeStruct((m, ATTN_WIDTH), BF16),
        jax.ShapeDtypeStruct((m, ATTN_WIDTH), BF16),
        jax.ShapeDtypeStruct((ATTN_WIDTH, m), BF16),
        jax.ShapeDtypeStruct((m, CONV_WIDTH), F32),
    ]
    if emit_w16:
        out_specs += [pl.BlockSpec((D_MODEL, tn), lambda i, j: (0, j))] * 5
        out_shape += [jax.ShapeDtypeStruct((D_MODEL, ATTN_WIDTH), BF16)] * 5
    outs = pl.pallas_call(
        functools.partial(_inproj_kernel, emit_w16=emit_w16),
        grid=(m // tm, nb),
        in_specs=[
            pl.BlockSpec((tm, D_MODEL), lambda i, j: (i, 0)),
            pl.BlockSpec((1, D_MODEL), lambda i, j: (0, 0)),
            *wspecs,
            pl.BlockSpec((1, LANES), lambda i, j: (0, 0)),
            pl.BlockSpec((1, LANES), lambda i, j: (0, 0)),
        ],
        out_specs=out_specs,
        out_shape=out_shape,
        scratch_shapes=[pltpu.VMEM((tm, D_MODEL), BF16)],
        compiler_params=_params("parallel", "arbitrary"),
        name="in_proj",
    )(x, g, *ws, qg, kg)
    return (outs[:4], outs[4:]) if emit_w16 else outs


def _attn_kernel(q_ref, k_ref, vt_ref, km_ref, vmt_ref, lq1_ref, lk1_ref, lq2_ref, lk2_ref, sg_ref,
                 *rest, tq, tk, cw, n_cast):
    w32_refs, o_ref, w16_refs = rest[:n_cast], rest[n_cast], rest[n_cast + 1:2 * n_cast + 1]
    m_sc, acc_sc, s0_sc, s1_sc = rest[2 * n_cast + 1:]
    n_chunks = 2 * tq // cw
    i = pl.program_id(2)
    qt = q_ref[...].astype(F32).T
    row = lax.broadcasted_iota(jnp.int32, qt.shape, 0)
    qs = jnp.concatenate([jnp.where(row < QK_DIM, qt, 0.0),
                          jnp.where(row < QK_DIM, 0.0, qt)], axis=1).astype(BF16)

    def scores(blk, s_ref):
        kb = k_ref[blk * tk:(blk + 1) * tk, :]
        for c in range(n_chunks):
            s_ref[c] = jnp.dot(kb, qs[:, c * cw:(c + 1) * cw], preferred_element_type=F32)

    def with_ones(vt):
        return jnp.concatenate([vt, jnp.ones((DENOM_ROWS, vt.shape[1]), BF16)], axis=0)

    def online_update(s, vt, m_old, acc_old):
        m_new = jnp.maximum(m_old, jnp.max(s, axis=0, keepdims=True))
        alpha = jnp.exp2(m_old - m_new)
        p = jnp.exp2(s - m_new)
        return m_new, alpha * acc_old + jnp.dot(with_ones(vt), p.astype(BF16), preferred_element_type=F32)

    def softmax_pv(blk, s_ref, diag):
        off = blk * tk
        for c in range(n_chunks):
            cs = slice(c * cw, (c + 1) * cw)
            nk, masked = tk, False
            if diag is not None:
                q0 = (c * cw) % tq
                nk = min(tk, q0 + cw - diag * tk)
                if nk <= 0:
                    continue
                masked = (diag * tk + nk - 1) // CHUNK > q0 // CHUNK
            s = s_ref[c, 0:nk, :]
            if masked:
                kchunk = (lax.broadcasted_iota(jnp.int32, (nk, cw), 0) + diag * tk) // CHUNK
                qchunk = (lax.broadcasted_iota(jnp.int32, (nk, cw), 1) + q0) // CHUNK
                s = jnp.where(kchunk <= qchunk, s, NEG_INF)
            m_sc[:, cs], acc_sc[:, cs] = online_update(s, vt_ref[:, off:off + nk], m_sc[:, cs], acc_sc[:, cs])

    def run(n_pairs):
        for w32, w16 in zip(w32_refs, w16_refs):
            w16[...] = w32[...].astype(BF16)
        scores(0, s0_sc)
        s = jnp.dot(km_ref[...], qs, preferred_element_type=F32)
        m_sc[...], acc_sc[...] = online_update(s, vmt_ref[...], jnp.full(m_sc.shape, NEG_INF, F32),
                                               jnp.zeros(acc_sc.shape, F32))
        for jp in range(n_pairs):
            scores(2 * jp + 1, s1_sc)
            softmax_pv(2 * jp, s0_sc, None)
            scores(2 * jp + 2, s0_sc)
            softmax_pv(2 * jp + 1, s1_sc, None)

        scores(2 * n_pairs + 1, s1_sc)
        softmax_pv(2 * n_pairs, s0_sc, 0)
        softmax_pv(2 * n_pairs + 1, s1_sc, 1)

        lam = (jnp.exp(jnp.sum(lq1_ref[...] * lk1_ref[...], axis=-1, keepdims=True))
               - jnp.exp(jnp.sum(lq2_ref[...] * lk2_ref[...], axis=-1, keepdims=True)) + LAM_INIT)
        acc = acc_sc[...]
        o = acc[:HEAD_DIM] * (1.0 / acc[HEAD_DIM:HEAD_DIM + 1])
        od = o[:, :tq] - lam * o[:, tq:]
        ms = jnp.mean(od * od, axis=0, keepdims=True)
        on = od * lax.rsqrt(ms + EPS) * sg_ref[...] * (1.0 - LAM_INIT)
        o_ref[...] = on.T.astype(BF16)

    for n in range(SEQ // tq):
        pl.when(i == n)(functools.partial(run, n))


def _attention(q, k, vt, km, vmt, lq1, lk1, lq2, lk2, sg, cast_ws, *, batch, tq, cw):
    tk = tq // 2
    assert tk % cw == 0 and cw % CHUNK == 0
    nq = SEQ // tq
    steps = batch * N_HEADS * nq
    lspec = pl.BlockSpec((1, QK_DIM), lambda b, h, i: (0, 0))

    def slab(w):
        rep = next(r for r in (1, 2, 4, 8) if w.shape[0] * r % (BF16_ROWS * steps) == 0)
        return pl.BlockSpec((w.shape[0] * rep // steps, w.shape[1]),
                            lambda b, h, i: (((b * N_HEADS + h) * nq + i) // rep, 0))

    outs = pl.pallas_call(
        functools.partial(_attn_kernel, tq=tq, tk=tk, cw=cw, n_cast=len(cast_ws)),
        grid=(batch, N_HEADS, nq),
        in_specs=[
            pl.BlockSpec((tq, HEAD_DIM), lambda b, h, i: (b * nq + i, h)),
            pl.BlockSpec((SEQ, HEAD_DIM), lambda b, h, i: (b, h)),
            pl.BlockSpec((HEAD_DIM, SEQ), lambda b, h, i: (h, b)),
            pl.BlockSpec((N_META, HEAD_DIM), lambda b, h, i: (0, h)),
            pl.BlockSpec((HEAD_DIM, N_META), lambda b, h, i: (h, 0)),
            lspec, lspec, lspec, lspec,
            pl.BlockSpec((HEAD_DIM, 1), lambda b, h, i: (0, 0)),
        ] + [slab(w) for w in cast_ws],
        out_specs=[pl.BlockSpec((tq, HEAD_DIM), lambda b, h, i: (b * nq + i, h))] + [slab(w) for w in cast_ws],
        out_shape=[jax.ShapeDtypeStruct((batch * SEQ, ATTN_WIDTH), BF16)]
        + [jax.ShapeDtypeStruct(w.shape, BF16) for w in cast_ws],
        scratch_shapes=[
            pltpu.VMEM((1, 2 * tq), F32),
            pltpu.VMEM((HEAD_DIM + DENOM_ROWS, 2 * tq), F32),
            pltpu.VMEM((2 * tq // cw, tk, cw), F32),
            pltpu.VMEM((2 * tq // cw, tk, cw), F32),
        ],
        compiler_params=_params("parallel", "parallel", "arbitrary"),
        name="diff_attn",
    )(q, k, vt, km, vmt, lq1, lk1, lq2, lk2, sg, *cast_ws)
    return outs[0], outs[1:]


def _conv_kernel(u_ref, prev_ref, halo0_ref, w_ref, b_ref, y_ref, win_sc, shift_sc, *, tl, rc, rb):
    i = pl.program_id(1)

    @pl.when(i == 0)
    def _():
        win_sc[0:HALO, :] = halo0_ref[...]

    @pl.when(i > 0)
    def _():
        win_sc[0:HALO, :] = prev_ref[...]

    win_sc[HALO:HALO + tl, :] = u_ref[...]

    nsh = HALO + tl - SUBLANES
    for s in range(1, SUBLANES):
        shift_sc[s - 1] = win_sc[s:s + nsh, :]

    first = HALO - (CONV_KERNEL - 1)
    @pl.loop(0, CONV_WIDTH // LANES)
    def _(c):
        cs = pl.ds(pl.multiple_of(c * LANES, LANES), LANES)
        for r0 in range(0, tl // rc, rb):
            accs = [jnp.broadcast_to(b_ref[:, cs], (rc, LANES))] * rb
            for t in range(CONV_KERNEL):
                wt = w_ref[t:t + 1, cs]
                for k in range(rb):
                    lo = (r0 + k) * rc + first + t
                    s, al = lo % SUBLANES, lo - lo % SUBLANES
                    tap = win_sc[al:al + rc, cs] if s == 0 else shift_sc[s - 1, al:al + rc, cs]
                    accs[k] = accs[k] + wt * tap
            for k in range(rb):
                y_ref[(r0 + k) * rc:(r0 + k + 1) * rc, cs] = accs[k]


def _conv(u, halo0, w, b, *, batch, tl, rc, rb):
    nt = SEQ // tl
    hb = tl // HALO
    return pl.pallas_call(
        functools.partial(_conv_kernel, tl=tl, rc=rc, rb=rb),
        grid=(batch, nt),
        in_specs=[
            pl.BlockSpec((tl, CONV_WIDTH), lambda bb, i: (bb * nt + i, 0)),
            pl.BlockSpec((HALO, CONV_WIDTH), lambda bb, i: (jnp.maximum((bb * nt + i) * hb - 1, 0), 0)),
            pl.BlockSpec((HALO, CONV_WIDTH), lambda bb, i: (0, 0)),
            pl.BlockSpec((CONV_KERNEL, CONV_WIDTH), lambda bb, i: (0, 0)),
            pl.BlockSpec((1, CONV_WIDTH), lambda bb, i: (0, 0)),
        ],
        out_specs=pl.BlockSpec((tl, CONV_WIDTH), lambda bb, i: (bb * nt + i, 0)),
        out_shape=jax.ShapeDtypeStruct((batch * SEQ, CONV_WIDTH), F32),
        scratch_shapes=[
            pltpu.VMEM((HALO + tl, CONV_WIDTH), F32),
            pltpu.VMEM((SUBLANES - 1, HALO + tl - SUBLANES, CONV_WIDTH), F32),
        ],
        compiler_params=_params("parallel", "arbitrary"),
        name="conv",
    )(u, u, halo0, w, b)


def _outproj_kernel(x_ref, a_ref, y_ref, cng_ref, wa_ref, wc_ref, o_ref):
    att = jnp.dot(a_ref[...], wa_ref[...], preferred_element_type=F32)
    yn = _rms(y_ref[...], cng_ref[...])
    oc = (yn * _sigmoid(yn)).astype(BF16)
    o_ref[...] = x_ref[...] + att + jnp.dot(oc, wc_ref[...], preferred_element_type=F32)


def _out_proj(x, oa, y, cng, w_out, *, tm):
    m = x.shape[0]
    return pl.pallas_call(
        _outproj_kernel,
        grid=(m // tm,),
        in_specs=[
            pl.BlockSpec((tm, D_MODEL), lambda i: (i, 0)),
            pl.BlockSpec((tm, ATTN_WIDTH), lambda i: (i, 0)),
            pl.BlockSpec((tm, CONV_WIDTH), lambda i: (i, 0)),
            pl.BlockSpec((1, CONV_WIDTH), lambda i: (0, 0)),
            pl.BlockSpec((ATTN_WIDTH, D_MODEL), lambda i: (0, 0)),
            pl.BlockSpec((CONV_WIDTH, D_MODEL), lambda i: (1, 0)),
        ],
        out_specs=pl.BlockSpec((tm, D_MODEL), lambda i: (i, 0)),
        out_shape=jax.ShapeDtypeStruct((m, D_MODEL), F32),
        compiler_params=_params("parallel"),
        name="out_proj",
    )(x, oa, y, cng, w_out, w_out)


def kernel(x, meta_tokens, ffn1_norm_g, ffn1_w_gate, ffn1_w_up, ffn1_w_down, mix_norm_g, w_in, q_norm_g,
           k_norm_g, lambda_q1, lambda_k1, lambda_q2, lambda_k2, attn_subln_g, conv_w, conv_b, conv_norm_g,
           w_out, ffn2_norm_g, ffn2_w_gate, ffn2_w_up, ffn2_w_down, final_norm_g):
    batch = x.shape[0]
    xr = x.reshape(batch * SEQ, D_MODEL)

    w1g, w1u, w1d = ffn1_w_gate[0], ffn1_w_up[0], ffn1_w_down[0]
    w2g, w2u, w2d = ffn2_w_gate[0], ffn2_w_up[0], ffn2_w_down[0]
    win, wout = w_in[0], w_out[0]
    qg = jnp.tile(q_norm_g[0] * (QK_DIM ** -0.5 * LOG2E), 2)[None, :]
    kg = jnp.tile(k_norm_g[0], 2)[None, :]
    sg = attn_subln_g[0][:, None]
    cw = conv_w[0].reshape(CONV_KERNEL, CONV_WIDTH)

    m1, (w1g, w1u, w1d) = _ffn(meta_tokens, ffn1_norm_g, w1g, w1u, w1d, final_norm_g,
                               tm=N_META, tf=512, final_norm=False, emit_w16=True)
    (_, km, vmt, um), win16 = _in_proj(m1, mix_norm_g, win, qg, kg, tm=N_META, tn=256, emit_w16=True)
    halo0 = jnp.concatenate([jnp.zeros((HALO - N_META, CONV_WIDTH), F32), um], axis=0)

    x1 = _ffn(xr, ffn1_norm_g, w1g, w1u, w1d, final_norm_g, tm=1024, tf=512, final_norm=False)
    q, k, vt, u = _in_proj(x1, mix_norm_g, win16, qg, kg, tm=1024, tn=256)
    oa, (w2g, w2u, w2d, wout) = _attention(q, k, vt, km, vmt, lambda_q1, lambda_k1, lambda_q2, lambda_k2, sg,
                                           [w2g, w2u, w2d, wout], batch=batch, tq=1024, cw=256)
    yc = _conv(u, halo0, cw, conv_b, batch=batch, tl=256, rc=32, rb=8)
    x2 = _out_proj(x1, oa, yc, conv_norm_g, wout, tm=512)
    y = _ffn(x2, ffn2_norm_g, w2g, w2u, w2d, final_norm_g, tm=1024, tf=512, final_norm=True)
    return y.reshape(batch, SEQ, D_MODEL)
```

```python
import functools
import math

import jax
import jax.numpy as jnp
from jax import lax
from jax.experimental import pallas as pl
from jax.experimental.pallas import tpu as pltpu

F32 = jnp.float32
BF16 = jnp.bfloat16

D_MODEL = 2048
SEQ = 4096
N_META = 16
CHUNK = 64
ATTN_WIDTH = 1024
CONV_WIDTH = 1024
HEAD_DIM = 128
N_HEADS = ATTN_WIDTH // HEAD_DIM
QK_DIM = HEAD_DIM // 2
CONV_KERNEL = 31
D_FF = 5632
EPS = 1e-6
NEG_INF = -1e30
LAM_INIT = 0.8 - 0.6 * math.exp(-0.3 * 1)

LOG2E = math.log2(math.e)

LANES = 128
SUBLANES = 8
BF16_ROWS = 16
DENOM_ROWS = 16
HALO = 32
VMEM_LIMIT = 56 * 1024 * 1024


def _params(*sem):
    return pltpu.CompilerParams(dimension_semantics=sem, vmem_limit_bytes=VMEM_LIMIT)


def _rms(x, g):
    ms = jnp.mean(x * x, axis=-1, keepdims=True)
    return x * lax.rsqrt(ms + EPS) * g


def _sigmoid(x):
    return 1.0 / (1.0 + jnp.exp(-x))


def _ffn_kernel(x_ref, g_ref, wg_ref, wu_ref, wd_ref, fg_ref, o_ref, *rest, final_norm, emit_w16):
    h_ref = rest[-1]
    j = pl.program_id(1)

    @pl.when(j == 0)
    def _():
        x = x_ref[...]
        h_ref[...] = _rms(x, g_ref[...]).astype(BF16)
        o_ref[...] = x

    wg, wu, wd = (w_ref[...].astype(BF16) for w_ref in (wg_ref, wu_ref, wd_ref))
    if emit_w16:
        for w16_ref, w in zip(rest[:3], (wg, wu, wd)):
            w16_ref[...] = w
    h = h_ref[...]
    gate = jnp.dot(h, wg, preferred_element_type=F32)
    up = jnp.dot(h, wu, preferred_element_type=F32)
    a = (gate * _sigmoid(gate)) * (up * 0.5)
    o_ref[...] += jnp.dot(a.astype(BF16), wd, preferred_element_type=F32)

    if final_norm:
        @pl.when(j == pl.num_programs(1) - 1)
        def _():
            o_ref[...] = _rms(o_ref[...], fg_ref[...])


def _ffn(x, g, wg, wu, wd, fg, *, tm, tf, final_norm, emit_w16=False):
    m = x.shape[0]
    grid = (m // tm, D_FF // tf)
    assert not emit_w16 or grid[0] == 1
    wspecs = [
        pl.BlockSpec((D_MODEL, tf), lambda i, j: (0, j)),
        pl.BlockSpec((D_MODEL, tf), lambda i, j: (0, j)),
        pl.BlockSpec((tf, D_MODEL), lambda i, j: (j, 0)),
    ]
    out_specs = [pl.BlockSpec((tm, D_MODEL), lambda i, j: (i, 0))]
    out_shape = [jax.ShapeDtypeStruct((m, D_MODEL), F32)]
    if emit_w16:
        out_specs += wspecs
        out_shape += [jax.ShapeDtypeStruct(w.shape, BF16) for w in (wg, wu, wd)]
    outs = pl.pallas_call(
        functools.partial(_ffn_kernel, final_norm=final_norm, emit_w16=emit_w16),
        grid=grid,
        in_specs=[
            pl.BlockSpec((tm, D_MODEL), lambda i, j: (i, 0)),
            pl.BlockSpec((1, D_MODEL), lambda i, j: (0, 0)),
            *wspecs,
            pl.BlockSpec((1, D_MODEL), lambda i, j: (0, 0)),
        ],
        out_specs=out_specs,
        out_shape=out_shape,
        scratch_shapes=[pltpu.VMEM((tm, D_MODEL), BF16)],
        compiler_params=_params("parallel", "arbitrary"),
        name="ffn_final" if final_norm else "ffn",
    )(x, g, wg, wu, wd, fg)
    return (outs[0], outs[1:]) if emit_w16 else outs[0]


def _qk_norm(x, g):
    outs = []
    for hh in range(x.shape[1] // LANES):
        xh = x[:, hh * LANES:(hh + 1) * LANES]
        lo = lax.broadcasted_iota(jnp.int32, xh.shape, 1) < QK_DIM
        x2 = xh * xh
        s_lo = jnp.sum(jnp.where(lo, x2, 0.0), axis=-1, keepdims=True)
        s_hi = jnp.sum(jnp.where(lo, 0.0, x2), axis=-1, keepdims=True)
        ms = jnp.where(lo, s_lo, s_hi) * (1.0 / QK_DIM)
        outs.append(xh * lax.rsqrt(ms + EPS) * g)
    return outs[0] if len(outs) == 1 else jnp.concatenate(outs, axis=1)


def _inproj_kernel(x_ref, g_ref, wq_ref, wk_ref, wv_ref, wa_ref, wc_ref, qg_ref, kg_ref,
                   q_ref, k_ref, vt_ref, u_ref, *rest, emit_w16):
    h_ref = rest[-1]
    j = pl.program_id(1)

    @pl.when(j == 0)
    def _():
        h_ref[...] = _rms(x_ref[...], g_ref[...]).astype(BF16)

    ws = [w_ref[...].astype(BF16) for w_ref in (wq_ref, wk_ref, wv_ref, wa_ref, wc_ref)]
    if emit_w16:
        for w16_ref, w in zip(rest[:5], ws):
            w16_ref[...] = w
    h = h_ref[...]
    q, k, v, ca, cg = (jnp.dot(h, w, preferred_element_type=F32) for w in ws)
    q_ref[...] = _qk_norm(q, qg_ref[...]).astype(BF16)
    k_ref[...] = _qk_norm(k, kg_ref[...]).astype(BF16)
    vt_ref[...] = v.T.astype(BF16)
    u_ref[...] = ca * _sigmoid(cg)


def _in_proj(x, g, ws, qg, kg, *, tm, tn, emit_w16=False):
    m = x.shape[0]
    nb = ATTN_WIDTH // tn
    assert not emit_w16 or m == tm
    if isinstance(ws, (tuple, list)):
        wspecs = [pl.BlockSpec((D_MODEL, tn), lambda i, j: (0, j))] * 5
    else:
        wspecs = [pl.BlockSpec((D_MODEL, tn), lambda i, j, c=c: (0, c * nb + j)) for c in range(5)]
        ws = [ws] * 5
    out_specs = [
        pl.BlockSpec((tm, tn), lambda i, j: (i, j)),
        pl.BlockSpec((tm, tn), lambda i, j: (i, j)),
        pl.BlockSpec((tn, tm), lambda i, j: (j, i)),
        pl.BlockSpec((tm, tn), lambda i, j: (i, j)),
    ]
    out_shape = [
        jax.ShapeDtypeStruct((m, ATTN_WIDTH), BF16),
        jax.ShapeDtypeStruct((m, ATTN_WIDTH), BF16),
        jax.ShapeDtypeStruct((ATTN_WIDTH, m), BF16),
        jax.ShapeDtypeStruct((m, CONV_WIDTH), F32),
    ]
    if emit_w16:
        out_specs += [pl.BlockSpec((D_MODEL, tn), lambda i, j: (0, j))] * 5
        out_shape += [jax.ShapeDtypeStruct((D_MODEL, ATTN_WIDTH), BF16)] * 5
    outs = pl.pallas_call(
        functools.partial(_inproj_kernel, emit_w16=emit_w16),
        grid=(m // tm, nb),
        in_specs=[
            pl.BlockSpec((tm, D_MODEL), lambda i, j: (i, 0)),
            pl.BlockSpec((1, D_MODEL), lambda i, j: (0, 0)),
            *wspecs,
            pl.BlockSpec((1, LANES), lambda i, j: (0, 0)),
            pl.BlockSpec((1, LANES), lambda i, j: (0, 0)),
        ],
        out_specs=out_specs,
        out_shape=out_shape,
        scratch_shapes=[pltpu.VMEM((tm, D_MODEL), BF16)],
        compiler_params=_params("parallel", "arbitrary"),
        name="in_proj",
    )(x, g, *ws, qg, kg)
    return (outs[:4], outs[4:]) if emit_w16 else outs


def _attn_kernel(q_ref, k_ref, vt_ref, km_ref, vmt_ref, lq1_ref, lk1_ref, lq2_ref, lk2_ref, sg_ref,
                 *rest, tq, tk, cw, n_cast):
    w32_refs, o_ref, w16_refs = rest[:n_cast], rest[n_cast], rest[n_cast + 1:2 * n_cast + 1]
    m_sc, acc_sc, s0_sc, s1_sc = rest[2 * n_cast + 1:]
    n_chunks = 2 * tq // cw
    i = pl.program_id(2)
    qt = q_ref[...].astype(F32).T
    row = lax.broadcasted_iota(jnp.int32, qt.shape, 0)
    qs = jnp.concatenate([jnp.where(row < QK_DIM, qt, 0.0),
                          jnp.where(row < QK_DIM, 0.0, qt)], axis=1).astype(BF16)

    def scores(blk, s_ref):
        kb = k_ref[blk * tk:(blk + 1) * tk, :]
        for c in range(n_chunks):
            s_ref[c] = jnp.dot(kb, qs[:, c * cw:(c + 1) * cw], preferred_element_type=F32)

    def with_ones(vt):
        return jnp.concatenate([vt, jnp.ones((DENOM_ROWS, vt.shape[1]), BF16)], axis=0)

    def online_update(s, vt, m_old, acc_old):
        m_new = jnp.maximum(m_old, jnp.max(s, axis=0, keepdims=True))
        alpha = jnp.exp2(m_old - m_new)
        p = jnp.exp2(s - m_new)
        return m_new, alpha * acc_old + jnp.dot(with_ones(vt), p.astype(BF16), preferred_element_type=F32)

    def softmax_pv(blk, s_ref, diag):
        off = blk * tk
        for c in range(n_chunks):
            cs = slice(c * cw, (c + 1) * cw)
            nk, masked = tk, False
            if diag is not None:
                q0 = (c * cw) % tq
                nk = min(tk, q0 + cw - diag * tk)
                if nk <= 0:
                    continue
                masked = (diag * tk + nk - 1) // CHUNK > q0 // CHUNK
            s = s_ref[c, 0:nk, :]
            if masked:
                kchunk = (lax.broadcasted_iota(jnp.int32, (nk, cw), 0) + diag * tk) // CHUNK
                qchunk = (lax.broadcasted_iota(jnp.int32, (nk, cw), 1) + q0) // CHUNK
                s = jnp.where(kchunk <= qchunk, s, NEG_INF)
            m_sc[:, cs], acc_sc[:, cs] = online_update(s, vt_ref[:, off:off + nk], m_sc[:, cs], acc_sc[:, cs])

    def run(tile):
        for w32, w16 in zip(w32_refs, w16_refs):
            w16[...] = w32[...].astype(BF16)
        per_tile = tq // tk
        blocks = [(b, None) for b in range(tile * per_tile)] + [(tile * per_tile + d, d) for d in range(per_tile)]
        bufs = (s0_sc, s1_sc)
        scores(blocks[0][0], bufs[0])
        s = jnp.dot(km_ref[...], qs, preferred_element_type=F32)
        m_sc[...], acc_sc[...] = online_update(s, vmt_ref[...], jnp.full(m_sc.shape, NEG_INF, F32),
                                               jnp.zeros(acc_sc.shape, F32))
        for t, (blk, diag) in enumerate(blocks):
            if t + 1 < len(blocks):
                scores(blocks[t + 1][0], bufs[(t + 1) % 2])
            softmax_pv(blk, bufs[t % 2], diag)

        lam = (jnp.exp(jnp.sum(lq1_ref[...] * lk1_ref[...], axis=-1, keepdims=True))
               - jnp.exp(jnp.sum(lq2_ref[...] * lk2_ref[...], axis=-1, keepdims=True)) + LAM_INIT)
        acc = acc_sc[...]
        o = acc[:HEAD_DIM] * (1.0 / acc[HEAD_DIM:HEAD_DIM + 1])
        od = o[:, :tq] - lam * o[:, tq:]
        ms = jnp.mean(od * od, axis=0, keepdims=True)
        on = od * lax.rsqrt(ms + EPS) * sg_ref[...] * (1.0 - LAM_INIT)
        o_ref[...] = on.T.astype(BF16)

    for n in range(SEQ // tq):
        pl.when(i == n)(functools.partial(run, n))


def _attention(q, k, vt, km, vmt, lq1, lk1, lq2, lk2, sg, cast_ws, *, batch, tq, tk, cw):
    assert tq % tk == 0 and tk % cw == 0 and cw % CHUNK == 0
    nq = SEQ // tq
    steps = batch * N_HEADS * nq
    lspec = pl.BlockSpec((1, QK_DIM), lambda b, h, i: (0, 0))

    def slab(w):
        rep = next(r for r in (1, 2, 4, 8) if w.shape[0] * r % (BF16_ROWS * steps) == 0)
        return pl.BlockSpec((w.shape[0] * rep // steps, w.shape[1]),
                            lambda b, h, i: (((b * N_HEADS + h) * nq + i) // rep, 0))

    outs = pl.pallas_call(
        functools.partial(_attn_kernel, tq=tq, tk=tk, cw=cw, n_cast=len(cast_ws)),
        grid=(batch, N_HEADS, nq),
        in_specs=[
            pl.BlockSpec((tq, HEAD_DIM), lambda b, h, i: (b * nq + i, h)),
            pl.BlockSpec((SEQ, HEAD_DIM), lambda b, h, i: (b, h)),
            pl.BlockSpec((HEAD_DIM, SEQ), lambda b, h, i: (h, b)),
            pl.BlockSpec((N_META, HEAD_DIM), lambda b, h, i: (0, h)),
            pl.BlockSpec((HEAD_DIM, N_META), lambda b, h, i: (h, 0)),
            lspec, lspec, lspec, lspec,
            pl.BlockSpec((HEAD_DIM, 1), lambda b, h, i: (0, 0)),
        ] + [slab(w) for w in cast_ws],
        out_specs=[pl.BlockSpec((tq, HEAD_DIM), lambda b, h, i: (b * nq + i, h))] + [slab(w) for w in cast_ws],
        out_shape=[jax.ShapeDtypeStruct((batch * SEQ, ATTN_WIDTH), BF16)]
        + [jax.ShapeDtypeStruct(w.shape, BF16) for w in cast_ws],
        scratch_shapes=[
            pltpu.VMEM((1, 2 * tq), F32),
            pltpu.VMEM((HEAD_DIM + DENOM_ROWS, 2 * tq), F32),
            pltpu.VMEM((2 * tq // cw, tk, cw), F32),
            pltpu.VMEM((2 * tq // cw, tk, cw), F32),
        ],
        compiler_params=_params("parallel", "parallel", "arbitrary"),
        name="diff_attn",
    )(q, k, vt, km, vmt, lq1, lk1, lq2, lk2, sg, *cast_ws)
    return outs[0], outs[1:]


def _conv_kernel(u_ref, prev_ref, halo0_ref, w_ref, b_ref, y_ref, win_sc, shift_sc, *, tl, rc, rb):
    i = pl.program_id(1)

    @pl.when(i == 0)
    def _():
        win_sc[0:HALO, :] = halo0_ref[...]

    @pl.when(i > 0)
    def _():
        win_sc[0:HALO, :] = prev_ref[...]

    win_sc[HALO:HALO + tl, :] = u_ref[...]

    nsh = HALO + tl - SUBLANES
    for s in range(1, SUBLANES):
        shift_sc[s - 1] = win_sc[s:s + nsh, :]

    first = HALO - (CONV_KERNEL - 1)
    @pl.loop(0, CONV_WIDTH // LANES)
    def _(c):
        cs = pl.ds(pl.multiple_of(c * LANES, LANES), LANES)
        for r0 in range(0, tl // rc, rb):
            accs = [jnp.broadcast_to(b_ref[:, cs], (rc, LANES))] * rb
            for t in range(CONV_KERNEL):
                wt = w_ref[t:t + 1, cs]
                for k in range(rb):
                    lo = (r0 + k) * rc + first + t
                    s, al = lo % SUBLANES, lo - lo % SUBLANES
                    tap = win_sc[al:al + rc, cs] if s == 0 else shift_sc[s - 1, al:al + rc, cs]
                    accs[k] = accs[k] + wt * tap
            for k in range(rb):
                y_ref[(r0 + k) * rc:(r0 + k + 1) * rc, cs] = accs[k]


def _conv(u, halo0, w, b, *, batch, tl, rc, rb):
    nt = SEQ // tl
    hb = tl // HALO
    return pl.pallas_call(
        functools.partial(_conv_kernel, tl=tl, rc=rc, rb=rb),
        grid=(batch, nt),
        in_specs=[
            pl.BlockSpec((tl, CONV_WIDTH), lambda bb, i: (bb * nt + i, 0)),
            pl.BlockSpec((HALO, CONV_WIDTH), lambda bb, i: (jnp.maximum((bb * nt + i) * hb - 1, 0), 0)),
            pl.BlockSpec((HALO, CONV_WIDTH), lambda bb, i: (0, 0)),
            pl.BlockSpec((CONV_KERNEL, CONV_WIDTH), lambda bb, i: (0, 0)),
            pl.BlockSpec((1, CONV_WIDTH), lambda bb, i: (0, 0)),
        ],
        out_specs=pl.BlockSpec((tl, CONV_WIDTH), lambda bb, i: (bb * nt + i, 0)),
        out_shape=jax.ShapeDtypeStruct((batch * SEQ, CONV_WIDTH), F32),
        scratch_shapes=[
            pltpu.VMEM((HALO + tl, CONV_WIDTH), F32),
            pltpu.VMEM((SUBLANES - 1, HALO + tl - SUBLANES, CONV_WIDTH), F32),
        ],
        compiler_params=_params("parallel", "arbitrary"),
        name="conv",
    )(u, u, halo0, w, b)


def _outproj_kernel(x_ref, a_ref, y_ref, cng_ref, wa_ref, wc_ref, o_ref):
    att = jnp.dot(a_ref[...], wa_ref[...], preferred_element_type=F32)
    yn = _rms(y_ref[...], cng_ref[...])
    oc = (yn * _sigmoid(yn)).astype(BF16)
    o_ref[...] = x_ref[...] + att + jnp.dot(oc, wc_ref[...], preferred_element_type=F32)


def _out_proj(x, oa, y, cng, w_out, *, tm):
    m = x.shape[0]
    return pl.pallas_call(
        _outproj_kernel,
        grid=(m // tm,),
        in_specs=[
            pl.BlockSpec((tm, D_MODEL), lambda i: (i, 0)),
            pl.BlockSpec((tm, ATTN_WIDTH), lambda i: (i, 0)),
            pl.BlockSpec((tm, CONV_WIDTH), lambda i: (i, 0)),
            pl.BlockSpec((1, CONV_WIDTH), lambda i: (0, 0)),
            pl.BlockSpec((ATTN_WIDTH, D_MODEL), lambda i: (0, 0)),
            pl.BlockSpec((CONV_WIDTH, D_MODEL), lambda i: (1, 0)),
        ],
        out_specs=pl.BlockSpec((tm, D_MODEL), lambda i: (i, 0)),
        out_shape=jax.ShapeDtypeStruct((m, D_MODEL), F32),
        compiler_params=_params("parallel"),
        name="out_proj",
    )(x, oa, y, cng, w_out, w_out)


def kernel(x, meta_tokens, ffn1_norm_g, ffn1_w_gate, ffn1_w_up, ffn1_w_down, mix_norm_g, w_in, q_norm_g,
           k_norm_g, lambda_q1, lambda_k1, lambda_q2, lambda_k2, attn_subln_g, conv_w, conv_b, conv_norm_g,
           w_out, ffn2_norm_g, ffn2_w_gate, ffn2_w_up, ffn2_w_down, final_norm_g):
    batch = x.shape[0]
    xr = x.reshape(batch * SEQ, D_MODEL)

    w1g, w1u, w1d = ffn1_w_gate[0], ffn1_w_up[0], ffn1_w_down[0]
    w2g, w2u, w2d = ffn2_w_gate[0], ffn2_w_up[0], ffn2_w_down[0]
    win, wout = w_in[0], w_out[0]
    qg = jnp.tile(q_norm_g[0] * (QK_DIM ** -0.5 * LOG2E), 2)[None, :]
    kg = jnp.tile(k_norm_g[0], 2)[None, :]
    sg = attn_subln_g[0][:, None]
    cw = conv_w[0].reshape(CONV_KERNEL, CONV_WIDTH)

    m1, (w1g, w1u, w1d) = _ffn(meta_tokens, ffn1_norm_g, w1g, w1u, w1d, final_norm_g,
                               tm=N_META, tf=512, final_norm=False, emit_w16=True)
    (_, km, vmt, um), win16 = _in_proj(m1, mix_norm_g, win, qg, kg, tm=N_META, tn=256, emit_w16=True)
    halo0 = jnp.concatenate([jnp.zeros((HALO - N_META, CONV_WIDTH), F32), um], axis=0)

    x1 = _ffn(xr, ffn1_norm_g, w1g, w1u, w1d, final_norm_g, tm=1024, tf=512, final_norm=False)
    q, k, vt, u = _in_proj(x1, mix_norm_g, win16, qg, kg, tm=1024, tn=256)
    oa, (w2g, w2u, w2d, wout) = _attention(q, k, vt, km, vmt, lambda_q1, lambda_k1, lambda_q2, lambda_k2, sg,
                                           [w2g, w2u, w2d, wout], batch=batch, tq=1024, tk=512, cw=256)
    yc = _conv(u, halo0, cw, conv_b, batch=batch, tl=512, rc=32, rb=8)
    x2 = _out_proj(x1, oa, yc, conv_norm_g, wout, tm=512)
    y = _ffn(x2, ffn2_norm_g, w2g, w2u, w2d, final_norm_g, tm=1024, tf=512, final_norm=True)
    return y.reshape(batch, SEQ, D_MODEL)
```

```python
import functools
import math

import jax
import jax.numpy as jnp
from jax import lax
from jax.experimental import pallas as pl
from jax.experimental.pallas import tpu as pltpu

F32 = jnp.float32
BF16 = jnp.bfloat16

D_MODEL = 2048
SEQ = 4096
N_META = 16
CHUNK = 64
ATTN_WIDTH = 1024
CONV_WIDTH = 1024
HEAD_DIM = 128
N_HEADS = ATTN_WIDTH // HEAD_DIM
QK_DIM = HEAD_DIM // 2
CONV_KERNEL = 31
D_FF = 5632
EPS = 1e-6
NEG_INF = -1e30
LAM_INIT = 0.8 - 0.6 * math.exp(-0.3 * 1)

LOG2E = math.log2(math.e)

LANES = 128
SUBLANES = 8
BF16_ROWS = 16
DENOM_ROWS = 16
HALO = 32
VMEM_LIMIT = 56 * 1024 * 1024

FFN_TM, FFN_TF = 1024, 512
META_FFN_TF = 512
PROJ_TM, PROJ_TN = 1024, 256
ATTN_TQ, ATTN_TK, ATTN_CW = 1024, 512, 256
CONV_TL, CONV_RC, CONV_RB = 512, 32, 8
OUT_TM = 512


def _params(*sem):
    return pltpu.CompilerParams(dimension_semantics=sem, vmem_limit_bytes=VMEM_LIMIT)


def _rms(x, g):
    ms = jnp.mean(x * x, axis=-1, keepdims=True)
    return x * lax.rsqrt(ms + EPS) * g


def _sigmoid(x):
    return 1.0 / (1.0 + jnp.exp(-x))


def _ffn_kernel(x_ref, g_ref, wg_ref, wu_ref, wd_ref, fg_ref, o_ref, *rest, final_norm, emit_w16):
    h_ref = rest[-1]
    j = pl.program_id(1)

    @pl.when(j == 0)
    def _():
        x = x_ref[...]
        h_ref[...] = _rms(x, g_ref[...]).astype(BF16)
        o_ref[...] = x

    wg, wu, wd = (w_ref[...].astype(BF16) for w_ref in (wg_ref, wu_ref, wd_ref))
    if emit_w16:
        for w16_ref, w in zip(rest[:3], (wg, wu, wd)):
            w16_ref[...] = w
    h = h_ref[...]
    gate = jnp.dot(h, wg, preferred_element_type=F32)
    up = jnp.dot(h, wu, preferred_element_type=F32)
    a = (gate * _sigmoid(gate)) * (up * 0.5)
    o_ref[...] += jnp.dot(a.astype(BF16), wd, preferred_element_type=F32)

    if final_norm:
        @pl.when(j == pl.num_programs(1) - 1)
        def _():
            o_ref[...] = _rms(o_ref[...], fg_ref[...])


def _ffn(x, g, wg, wu, wd, fg, *, tm, tf, final_norm, emit_w16=False):
    m = x.shape[0]
    grid = (m // tm, D_FF // tf)
    assert not emit_w16 or grid[0] == 1
    wspecs = [
        pl.BlockSpec((D_MODEL, tf), lambda i, j: (0, j)),
        pl.BlockSpec((D_MODEL, tf), lambda i, j: (0, j)),
        pl.BlockSpec((tf, D_MODEL), lambda i, j: (j, 0)),
    ]
    out_specs = [pl.BlockSpec((tm, D_MODEL), lambda i, j: (i, 0))]
    out_shape = [jax.ShapeDtypeStruct((m, D_MODEL), F32)]
    if emit_w16:
        out_specs += wspecs
        out_shape += [jax.ShapeDtypeStruct(w.shape, BF16) for w in (wg, wu, wd)]
    outs = pl.pallas_call(
        functools.partial(_ffn_kernel, final_norm=final_norm, emit_w16=emit_w16),
        grid=grid,
        in_specs=[
            pl.BlockSpec((tm, D_MODEL), lambda i, j: (i, 0)),
            pl.BlockSpec((1, D_MODEL), lambda i, j: (0, 0)),
            *wspecs,
            pl.BlockSpec((1, D_MODEL), lambda i, j: (0, 0)),
        ],
        out_specs=out_specs,
        out_shape=out_shape,
        scratch_shapes=[pltpu.VMEM((tm, D_MODEL), BF16)],
        compiler_params=_params("parallel", "arbitrary"),
        name="ffn_final" if final_norm else "ffn",
    )(x, g, wg, wu, wd, fg)
    return (outs[0], outs[1:]) if emit_w16 else outs[0]


def _qk_norm(x, g):
    outs = []
    for hh in range(x.shape[1] // LANES):
        xh = x[:, hh * LANES:(hh + 1) * LANES]
        lo = lax.broadcasted_iota(jnp.int32, xh.shape, 1) < QK_DIM
        x2 = xh * xh
        s_lo = jnp.sum(jnp.where(lo, x2, 0.0), axis=-1, keepdims=True)
        s_hi = jnp.sum(jnp.where(lo, 0.0, x2), axis=-1, keepdims=True)
        ms = jnp.where(lo, s_lo, s_hi) * (1.0 / QK_DIM)
        outs.append(xh * lax.rsqrt(ms + EPS) * g)
    return outs[0] if len(outs) == 1 else jnp.concatenate(outs, axis=1)


def _inproj_kernel(x_ref, g_ref, wq_ref, wk_ref, wv_ref, wa_ref, wc_ref, qg_ref, kg_ref,
                   q_ref, k_ref, vt_ref, u_ref, *rest, emit_w16):
    h_ref = rest[-1]
    j = pl.program_id(1)

    @pl.when(j == 0)
    def _():
        h_ref[...] = _rms(x_ref[...], g_ref[...]).astype(BF16)

    ws = [w_ref[...].astype(BF16) for w_ref in (wq_ref, wk_ref, wv_ref, wa_ref, wc_ref)]
    if emit_w16:
        for w16_ref, w in zip(rest[:5], ws):
            w16_ref[...] = w
    h = h_ref[...]
    q, k, v, ca, cg = (jnp.dot(h, w, preferred_element_type=F32) for w in ws)
    q_ref[...] = _qk_norm(q, qg_ref[...]).astype(BF16)
    k_ref[...] = _qk_norm(k, kg_ref[...]).astype(BF16)
    vt_ref[...] = v.T.astype(BF16)
    u_ref[...] = ca * _sigmoid(cg)


def _in_proj(x, g, ws, qg, kg, *, tm, tn, emit_w16=False):
    m = x.shape[0]
    nb = ATTN_WIDTH // tn
    assert not emit_w16 or m == tm
    if isinstance(ws, (tuple, list)):
        wspecs = [pl.BlockSpec((D_MODEL, tn), lambda i, j: (0, j))] * 5
    else:
        wspecs = [pl.BlockSpec((D_MODEL, tn), lambda i, j, c=c: (0, c * nb + j)) for c in range(5)]
        ws = [ws] * 5
    out_specs = [
        pl.BlockSpec((tm, tn), lambda i, j: (i, j)),
        pl.BlockSpec((tm, tn), lambda i, j: (i, j)),
        pl.BlockSpec((tn, tm), lambda i, j: (j, i)),
        pl.BlockSpec((tm, tn), lambda i, j: (i, j)),
    ]
    out_shape = [
        jax.ShapeDtypeStruct((m, ATTN_WIDTH), BF16),
        jax.ShapeDtypeStruct((m, ATTN_WIDTH), BF16),
        jax.ShapeDtypeStruct((ATTN_WIDTH, m), BF16),
        jax.ShapeDtypeStruct((m, CONV_WIDTH), F32),
    ]
    if emit_w16:
        out_specs += [pl.BlockSpec((D_MODEL, tn), lambda i, j: (0, j))] * 5
        out_shape += [jax.ShapeDtypeStruct((D_MODEL, ATTN_WIDTH), BF16)] * 5
    outs = pl.pallas_call(
        functools.partial(_inproj_kernel, emit_w16=emit_w16),
        grid=(m // tm, nb),
        in_specs=[
            pl.BlockSpec((tm, D_MODEL), lambda i, j: (i, 0)),
            pl.BlockSpec((1, D_MODEL), lambda i, j: (0, 0)),
            *wspecs,
            pl.BlockSpec((1, LANES), lambda i, j: (0, 0)),
            pl.BlockSpec((1, LANES), lambda i, j: (0, 0)),
        ],
        out_specs=out_specs,
        out_shape=out_shape,
        scratch_shapes=[pltpu.VMEM((tm, D_MODEL), BF16)],
        compiler_params=_params("parallel", "arbitrary"),
        name="in_proj",
    )(x, g, *ws, qg, kg)
    return (outs[:4], outs[4:]) if emit_w16 else outs


def _attn_kernel(q_ref, k_ref, vt_ref, km_ref, vmt_ref, lq1_ref, lk1_ref, lq2_ref, lk2_ref, sg_ref,
                 *rest, tq, tk, cw, n_cast):
    w32_refs, o_ref, w16_refs = rest[:n_cast], rest[n_cast], rest[n_cast + 1:2 * n_cast + 1]
    m_sc, acc_sc, s0_sc, s1_sc = rest[2 * n_cast + 1:]
    n_chunks = 2 * tq // cw
    i = pl.program_id(2)
    qt = q_ref[...].astype(F32).T
    row = lax.broadcasted_iota(jnp.int32, qt.shape, 0)
    qs = jnp.concatenate([jnp.where(row < QK_DIM, qt, 0.0),
                          jnp.where(row < QK_DIM, 0.0, qt)], axis=1).astype(BF16)

    def scores(blk, s_ref):
        kb = k_ref[blk * tk:(blk + 1) * tk, :]
        for c in range(n_chunks):
            s_ref[c] = jnp.dot(kb, qs[:, c * cw:(c + 1) * cw], preferred_element_type=F32)

    def with_ones(vt):
        return jnp.concatenate([vt, jnp.ones((DENOM_ROWS, vt.shape[1]), BF16)], axis=0)

    def online_update(s, vt, m_old, acc_old):
        m_new = jnp.maximum(m_old, jnp.max(s, axis=0, keepdims=True))
        alpha = jnp.exp2(m_old - m_new)
        p = jnp.exp2(s - m_new)
        return m_new, alpha * acc_old + jnp.dot(with_ones(vt), p.astype(BF16), preferred_element_type=F32)

    def softmax_pv(blk, s_ref, diag):
        off = blk * tk
        for c in range(n_chunks):
            cs = slice(c * cw, (c + 1) * cw)
            nk, masked = tk, False
            if diag is not None:
                q0 = (c * cw) % tq
                nk = min(tk, q0 + cw - diag * tk)
                if nk <= 0:
                    continue
                masked = (diag * tk + nk - 1) // CHUNK > q0 // CHUNK
            s = s_ref[c, 0:nk, :]
            if masked:
                kchunk = (lax.broadcasted_iota(jnp.int32, (nk, cw), 0) + diag * tk) // CHUNK
                qchunk = (lax.broadcasted_iota(jnp.int32, (nk, cw), 1) + q0) // CHUNK
                s = jnp.where(kchunk <= qchunk, s, NEG_INF)
            m_sc[:, cs], acc_sc[:, cs] = online_update(s, vt_ref[:, off:off + nk], m_sc[:, cs], acc_sc[:, cs])

    def run(tile):
        for w32, w16 in zip(w32_refs, w16_refs):
            w16[...] = w32[...].astype(BF16)
        per_tile = tq // tk
        blocks = [(b, None) for b in range(tile * per_tile)] + [(tile * per_tile + d, d) for d in range(per_tile)]
        bufs = (s0_sc, s1_sc)
        scores(blocks[0][0], bufs[0])
        s = jnp.dot(km_ref[...], qs, preferred_element_type=F32)
        m_sc[...], acc_sc[...] = online_update(s, vmt_ref[...], jnp.full(m_sc.shape, NEG_INF, F32),
                                               jnp.zeros(acc_sc.shape, F32))
        for t, (blk, diag) in enumerate(blocks):
            if t + 1 < len(blocks):
                scores(blocks[t + 1][0], bufs[(t + 1) % 2])
            softmax_pv(blk, bufs[t % 2], diag)

        lam = (jnp.exp(jnp.sum(lq1_ref[...] * lk1_ref[...], axis=-1, keepdims=True))
               - jnp.exp(jnp.sum(lq2_ref[...] * lk2_ref[...], axis=-1, keepdims=True)) + LAM_INIT)
        acc = acc_sc[...]
        o = acc[:HEAD_DIM] * (1.0 / acc[HEAD_DIM:HEAD_DIM + 1])
        od = o[:, :tq] - lam * o[:, tq:]
        ms = jnp.mean(od * od, axis=0, keepdims=True)
        on = od * lax.rsqrt(ms + EPS) * sg_ref[...] * (1.0 - LAM_INIT)
        o_ref[...] = on.T.astype(BF16)

    for n in range(SEQ // tq):
        pl.when(i == n)(functools.partial(run, n))


def _attention(q, k, vt, km, vmt, lq1, lk1, lq2, lk2, sg, cast_ws, *, batch, tq, tk, cw):
    assert tq % tk == 0 and tk % cw == 0 and cw % CHUNK == 0
    nq = SEQ // tq
    steps = batch * N_HEADS * nq
    lspec = pl.BlockSpec((1, QK_DIM), lambda b, h, i: (0, 0))

    def slab(w):
        rep = next(r for r in (1, 2, 4, 8) if w.shape[0] * r % (BF16_ROWS * steps) == 0)
        return pl.BlockSpec((w.shape[0] * rep // steps, w.shape[1]),
                            lambda b, h, i: (((b * N_HEADS + h) * nq + i) // rep, 0))

    outs = pl.pallas_call(
        functools.partial(_attn_kernel, tq=tq, tk=tk, cw=cw, n_cast=len(cast_ws)),
        grid=(batch, N_HEADS, nq),
        in_specs=[
            pl.BlockSpec((tq, HEAD_DIM), lambda b, h, i: (b * nq + i, h)),
            pl.BlockSpec((SEQ, HEAD_DIM), lambda b, h, i: (b, h)),
            pl.BlockSpec((HEAD_DIM, SEQ), lambda b, h, i: (h, b)),
            pl.BlockSpec((N_META, HEAD_DIM), lambda b, h, i: (0, h)),
            pl.BlockSpec((HEAD_DIM, N_META), lambda b, h, i: (h, 0)),
            lspec, lspec, lspec, lspec,
            pl.BlockSpec((HEAD_DIM, 1), lambda b, h, i: (0, 0)),
        ] + [slab(w) for w in cast_ws],
        out_specs=[pl.BlockSpec((tq, HEAD_DIM), lambda b, h, i: (b * nq + i, h))] + [slab(w) for w in cast_ws],
        out_shape=[jax.ShapeDtypeStruct((batch * SEQ, ATTN_WIDTH), BF16)]
        + [jax.ShapeDtypeStruct(w.shape, BF16) for w in cast_ws],
        scratch_shapes=[
            pltpu.VMEM((1, 2 * tq), F32),
            pltpu.VMEM((HEAD_DIM + DENOM_ROWS, 2 * tq), F32),
            pltpu.VMEM((2 * tq // cw, tk, cw), F32),
            pltpu.VMEM((2 * tq // cw, tk, cw), F32),
        ],
        compiler_params=_params("parallel", "parallel", "arbitrary"),
        name="diff_attn",
    )(q, k, vt, km, vmt, lq1, lk1, lq2, lk2, sg, *cast_ws)
    return outs[0], outs[1:]


def _conv_kernel(u_ref, prev_ref, halo0_ref, w_ref, b_ref, y_ref, win_sc, shift_sc, *, tl, rc, rb):
    i = pl.program_id(1)

    @pl.when(i == 0)
    def _():
        win_sc[0:HALO, :] = halo0_ref[...]

    @pl.when(i > 0)
    def _():
        win_sc[0:HALO, :] = prev_ref[...]

    win_sc[HALO:HALO + tl, :] = u_ref[...]

    nsh = HALO + tl - SUBLANES
    for s in range(1, SUBLANES):
        shift_sc[s - 1] = win_sc[s:s + nsh, :]

    first = HALO - (CONV_KERNEL - 1)
    @pl.loop(0, CONV_WIDTH // LANES)
    def _(c):
        cs = pl.ds(pl.multiple_of(c * LANES, LANES), LANES)
        for r0 in range(0, tl // rc, rb):
            accs = [jnp.broadcast_to(b_ref[:, cs], (rc, LANES))] * rb
            for t in range(CONV_KERNEL):
                wt = w_ref[t:t + 1, cs]
                for k in range(rb):
                    lo = (r0 + k) * rc + first + t
                    s, al = lo % SUBLANES, lo - lo % SUBLANES
                    tap = win_sc[al:al + rc, cs] if s == 0 else shift_sc[s - 1, al:al + rc, cs]
                    accs[k] = accs[k] + wt * tap
            for k in range(rb):
                y_ref[(r0 + k) * rc:(r0 + k + 1) * rc, cs] = accs[k]


def _conv(u, halo0, w, b, *, batch, tl, rc, rb):
    nt = SEQ // tl
    hb = tl // HALO
    return pl.pallas_call(
        functools.partial(_conv_kernel, tl=tl, rc=rc, rb=rb),
        grid=(batch, nt),
        in_specs=[
            pl.BlockSpec((tl, CONV_WIDTH), lambda bb, i: (bb * nt + i, 0)),
            pl.BlockSpec((HALO, CONV_WIDTH), lambda bb, i: (jnp.maximum((bb * nt + i) * hb - 1, 0), 0)),
            pl.BlockSpec((HALO, CONV_WIDTH), lambda bb, i: (0, 0)),
            pl.BlockSpec((CONV_KERNEL, CONV_WIDTH), lambda bb, i: (0, 0)),
            pl.BlockSpec((1, CONV_WIDTH), lambda bb, i: (0, 0)),
        ],
        out_specs=pl.BlockSpec((tl, CONV_WIDTH), lambda bb, i: (bb * nt + i, 0)),
        out_shape=jax.ShapeDtypeStruct((batch * SEQ, CONV_WIDTH), F32),
        scratch_shapes=[
            pltpu.VMEM((HALO + tl, CONV_WIDTH), F32),
            pltpu.VMEM((SUBLANES - 1, HALO + tl - SUBLANES, CONV_WIDTH), F32),
        ],
        compiler_params=_params("parallel", "arbitrary"),
        name="conv",
    )(u, u, halo0, w, b)


def _outproj_kernel(x_ref, a_ref, y_ref, cng_ref, wa_ref, wc_ref, o_ref):
    att = jnp.dot(a_ref[...], wa_ref[...], preferred_element_type=F32)
    yn = _rms(y_ref[...], cng_ref[...])
    oc = (yn * _sigmoid(yn)).astype(BF16)
    o_ref[...] = x_ref[...] + att + jnp.dot(oc, wc_ref[...], preferred_element_type=F32)


def _out_proj(x, oa, y, cng, w_out, *, tm):
    m = x.shape[0]
    return pl.pallas_call(
        _outproj_kernel,
        grid=(m // tm,),
        in_specs=[
            pl.BlockSpec((tm, D_MODEL), lambda i: (i, 0)),
            pl.BlockSpec((tm, ATTN_WIDTH), lambda i: (i, 0)),
            pl.BlockSpec((tm, CONV_WIDTH), lambda i: (i, 0)),
            pl.BlockSpec((1, CONV_WIDTH), lambda i: (0, 0)),
            pl.BlockSpec((ATTN_WIDTH, D_MODEL), lambda i: (0, 0)),
            pl.BlockSpec((CONV_WIDTH, D_MODEL), lambda i: (1, 0)),
        ],
        out_specs=pl.BlockSpec((tm, D_MODEL), lambda i: (i, 0)),
        out_shape=jax.ShapeDtypeStruct((m, D_MODEL), F32),
        compiler_params=_params("parallel"),
        name="out_proj",
    )(x, oa, y, cng, w_out, w_out)


def kernel(x, meta_tokens, ffn1_norm_g, ffn1_w_gate, ffn1_w_up, ffn1_w_down, mix_norm_g, w_in, q_norm_g,
           k_norm_g, lambda_q1, lambda_k1, lambda_q2, lambda_k2, attn_subln_g, conv_w, conv_b, conv_norm_g,
           w_out, ffn2_norm_g, ffn2_w_gate, ffn2_w_up, ffn2_w_down, final_norm_g):
    batch = x.shape[0]
    xr = x.reshape(batch * SEQ, D_MODEL)

    w1g, w1u, w1d = ffn1_w_gate[0], ffn1_w_up[0], ffn1_w_down[0]
    w2g, w2u, w2d = ffn2_w_gate[0], ffn2_w_up[0], ffn2_w_down[0]
    win, wout = w_in[0], w_out[0]
    qg = jnp.tile(q_norm_g[0] * (QK_DIM ** -0.5 * LOG2E), 2)[None, :]
    kg = jnp.tile(k_norm_g[0], 2)[None, :]
    sg = attn_subln_g[0][:, None]
    cw = conv_w[0].reshape(CONV_KERNEL, CONV_WIDTH)

    m1, (w1g, w1u, w1d) = _ffn(meta_tokens, ffn1_norm_g, w1g, w1u, w1d, final_norm_g,
                               tm=N_META, tf=META_FFN_TF, final_norm=False, emit_w16=True)
    (_, km, vmt, um), win16 = _in_proj(m1, mix_norm_g, win, qg, kg, tm=N_META, tn=PROJ_TN, emit_w16=True)
    halo0 = jnp.concatenate([jnp.zeros((HALO - N_META, CONV_WIDTH), F32), um], axis=0)

    x1 = _ffn(xr, ffn1_norm_g, w1g, w1u, w1d, final_norm_g, tm=FFN_TM, tf=FFN_TF, final_norm=False)
    q, k, vt, u = _in_proj(x1, mix_norm_g, win16, qg, kg, tm=PROJ_TM, tn=PROJ_TN)
    oa, (w2g, w2u, w2d, wout) = _attention(q, k, vt, km, vmt, lambda_q1, lambda_k1, lambda_q2, lambda_k2, sg,
                                           [w2g, w2u, w2d, wout], batch=batch,
                                           tq=ATTN_TQ, tk=ATTN_TK, cw=ATTN_CW)
    yc = _conv(u, halo0, cw, conv_b, batch=batch, tl=CONV_TL, rc=CONV_RC, rb=CONV_RB)
    x2 = _out_proj(x1, oa, yc, conv_norm_g, wout, tm=OUT_TM)
    y = _ffn(x2, ffn2_norm_g, w2g, w2u, w2d, final_norm_g, tm=FFN_TM, tf=FFN_TF, final_norm=True)
    return y.reshape(batch, SEQ, D_MODEL)
```

```python
import functools
import math

import jax
import jax.numpy as jnp
from jax import lax
from jax.experimental import pallas as pl
from jax.experimental.pallas import tpu as pltpu

F32 = jnp.float32
BF16 = jnp.bfloat16

D_MODEL = 2048
SEQ = 4096
N_META = 16
CHUNK = 64
ATTN_WIDTH = 1024
CONV_WIDTH = 1024
HEAD_DIM = 128
N_HEADS = ATTN_WIDTH // HEAD_DIM
QK_DIM = HEAD_DIM // 2
CONV_KERNEL = 31
D_FF = 5632
EPS = 1e-6
NEG_INF = -1e30
LAM_INIT = 0.8 - 0.6 * math.exp(-0.3 * 1)

LOG2E = math.log2(math.e)

LANES = 128
SUBLANES = 8
BF16_ROWS = 16
DENOM_ROWS = 16
HALO = 32
VMEM_LIMIT = 56 * 1024 * 1024

FFN_TM, FFN_TF = 1024, 512
META_FFN_TF = 512
PROJ_TM, PROJ_TN = 1024, 256
ATTN_TQ, ATTN_TK, ATTN_CW = 1024, 512, 256
CONV_TL, CONV_RC, CONV_RB = 512, 32, 8
OUT_TM = 512


def _params(*sem):
    return pltpu.CompilerParams(dimension_semantics=sem, vmem_limit_bytes=VMEM_LIMIT)


def _rms(x, g):
    ms = jnp.mean(x * x, axis=-1, keepdims=True)
    return x * lax.rsqrt(ms + EPS) * g


def _sigmoid(x):
    return 1.0 / (1.0 + jnp.exp(-x))


def _ffn_kernel(x_ref, g_ref, wg_ref, wu_ref, wd_ref, fg_ref, o_ref, *rest, final_norm, emit_w16):
    h_ref = rest[-1]
    j = pl.program_id(1)

    @pl.when(j == 0)
    def _():
        x = x_ref[...]
        h_ref[...] = _rms(x, g_ref[...]).astype(BF16)
        o_ref[...] = x

    wg, wu, wd = (w_ref[...].astype(BF16) for w_ref in (wg_ref, wu_ref, wd_ref))
    if emit_w16:
        for w16_ref, w in zip(rest[:3], (wg, wu, wd)):
            w16_ref[...] = w
    h = h_ref[...]
    gate = jnp.dot(h, wg, preferred_element_type=F32)
    up = jnp.dot(h, wu, preferred_element_type=F32)
    a = (gate * _sigmoid(gate)) * (up * 0.5)
    o_ref[...] += jnp.dot(a.astype(BF16), wd, preferred_element_type=F32)

    if final_norm:
        @pl.when(j == pl.num_programs(1) - 1)
        def _():
            o_ref[...] = _rms(o_ref[...], fg_ref[...])


def _ffn(x, g, wg, wu, wd, fg, *, tm, tf, final_norm, emit_w16=False):
    m = x.shape[0]
    grid = (m // tm, D_FF // tf)
    assert not emit_w16 or grid[0] == 1
    wspecs = [
        pl.BlockSpec((D_MODEL, tf), lambda i, j: (0, j)),
        pl.BlockSpec((D_MODEL, tf), lambda i, j: (0, j)),
        pl.BlockSpec((tf, D_MODEL), lambda i, j: (j, 0)),
    ]
    out_specs = [pl.BlockSpec((tm, D_MODEL), lambda i, j: (i, 0))]
    out_shape = [jax.ShapeDtypeStruct((m, D_MODEL), F32)]
    if emit_w16:
        out_specs += wspecs
        out_shape += [jax.ShapeDtypeStruct(w.shape, BF16) for w in (wg, wu, wd)]
    outs = pl.pallas_call(
        functools.partial(_ffn_kernel, final_norm=final_norm, emit_w16=emit_w16),
        grid=grid,
        in_specs=[
            pl.BlockSpec((tm, D_MODEL), lambda i, j: (i, 0)),
            pl.BlockSpec((1, D_MODEL), lambda i, j: (0, 0)),
            *wspecs,
            pl.BlockSpec((1, D_MODEL), lambda i, j: (0, 0)),
        ],
        out_specs=out_specs,
        out_shape=out_shape,
        scratch_shapes=[pltpu.VMEM((tm, D_MODEL), BF16)],
        compiler_params=_params("parallel", "arbitrary"),
        name="ffn_final" if final_norm else "ffn",
    )(x, g, wg, wu, wd, fg)
    return (outs[0], outs[1:]) if emit_w16 else outs[0]


def _qk_norm(x, g):
    outs = []
    for hh in range(x.shape[1] // LANES):
        xh = x[:, hh * LANES:(hh + 1) * LANES]
        lo = lax.broadcasted_iota(jnp.int32, xh.shape, 1) < QK_DIM
        x2 = xh * xh
        s_lo = jnp.sum(jnp.where(lo, x2, 0.0), axis=-1, keepdims=True)
        s_hi = jnp.sum(jnp.where(lo, 0.0, x2), axis=-1, keepdims=True)
        ms = jnp.where(lo, s_lo, s_hi) * (1.0 / QK_DIM)
        outs.append(xh * lax.rsqrt(ms + EPS) * g)
    return outs[0] if len(outs) == 1 else jnp.concatenate(outs, axis=1)


def _inproj_kernel(x_ref, g_ref, wq_ref, wk_ref, wv_ref, wa_ref, wc_ref, qg_ref, kg_ref,
                   q_ref, k_ref, vt_ref, u_ref, *rest, emit_w16):
    h_ref = rest[-1]
    j = pl.program_id(1)

    @pl.when(j == 0)
    def _():
        h_ref[...] = _rms(x_ref[...], g_ref[...]).astype(BF16)

    ws = [w_ref[...].astype(BF16) for w_ref in (wq_ref, wk_ref, wv_ref, wa_ref, wc_ref)]
    if emit_w16:
        for w16_ref, w in zip(rest[:5], ws):
            w16_ref[...] = w
    h = h_ref[...]
    q, k, v, ca, cg = (jnp.dot(h, w, preferred_element_type=F32) for w in ws)
    q_ref[...] = _qk_norm(q, qg_ref[...]).astype(BF16)
    k_ref[...] = _qk_norm(k, kg_ref[...]).astype(BF16)
    vt_ref[...] = v.T.astype(BF16)
    u_ref[...] = ca * _sigmoid(cg)


def _in_proj(x, g, ws, qg, kg, *, tm, tn, emit_w16=False):
    m = x.shape[0]
    nb = ATTN_WIDTH // tn
    assert not emit_w16 or m == tm
    if isinstance(ws, (tuple, list)):
        wspecs = [pl.BlockSpec((D_MODEL, tn), lambda i, j: (0, j))] * 5
    else:
        wspecs = [pl.BlockSpec((D_MODEL, tn), lambda i, j, c=c: (0, c * nb + j)) for c in range(5)]
        ws = [ws] * 5
    out_specs = [
        pl.BlockSpec((tm, tn), lambda i, j: (i, j)),
        pl.BlockSpec((tm, tn), lambda i, j: (i, j)),
        pl.BlockSpec((tn, tm), lambda i, j: (j, i)),
        pl.BlockSpec((tm, tn), lambda i, j: (i, j)),
    ]
    out_shape = [
        jax.ShapeDtypeStruct((m, ATTN_WIDTH), BF16),
        jax.ShapeDtypeStruct((m, ATTN_WIDTH), BF16),
        jax.ShapeDtypeStruct((ATTN_WIDTH, m), BF16),
        jax.ShapeDtypeStruct((m, CONV_WIDTH), F32),
    ]
    if emit_w16:
        out_specs += [pl.BlockSpec((D_MODEL, tn), lambda i, j: (0, j))] * 5
        out_shape += [jax.ShapeDtypeStruct((D_MODEL, ATTN_WIDTH), BF16)] * 5
    outs = pl.pallas_call(
        functools.partial(_inproj_kernel, emit_w16=emit_w16),
        grid=(m // tm, nb),
        in_specs=[
            pl.BlockSpec((tm, D_MODEL), lambda i, j: (i, 0)),
            pl.BlockSpec((1, D_MODEL), lambda i, j: (0, 0)),
            *wspecs,
            pl.BlockSpec((1, LANES), lambda i, j: (0, 0)),
            pl.BlockSpec((1, LANES), lambda i, j: (0, 0)),
        ],
        out_specs=out_specs,
        out_shape=out_shape,
        scratch_shapes=[pltpu.VMEM((tm, D_MODEL), BF16)],
        compiler_params=_params("parallel", "arbitrary"),
        name="in_proj",
    )(x, g, *ws, qg, kg)
    return (outs[:4], outs[4:]) if emit_w16 else outs


def _attn_kernel(q_ref, k_ref, vt_ref, km_ref, vmt_ref, lq1_ref, lk1_ref, lq2_ref, lk2_ref, sg_ref,
                 *rest, tq, tk, cw, n_cast, tiles_per_step):
    w32_refs, o_ref, w16_refs = rest[:n_cast], rest[n_cast], rest[n_cast + 1:2 * n_cast + 1]
    m_sc, acc_sc, s0_sc, s1_sc = rest[2 * n_cast + 1:]
    n_chunks = 2 * tq // cw
    i = pl.program_id(2)

    def scores(blk, s_ref, qs):
        kb = k_ref[blk * tk:(blk + 1) * tk, :]
        for c in range(n_chunks):
            s_ref[c] = jnp.dot(kb, qs[:, c * cw:(c + 1) * cw], preferred_element_type=F32)

    def with_ones(vt):
        return jnp.concatenate([vt, jnp.ones((DENOM_ROWS, vt.shape[1]), BF16)], axis=0)

    def online_update(s, vt, m_old, acc_old):
        m_new = jnp.maximum(m_old, jnp.max(s, axis=0, keepdims=True))
        alpha = jnp.exp2(m_old - m_new)
        p = jnp.exp2(s - m_new)
        return m_new, alpha * acc_old + jnp.dot(with_ones(vt), p.astype(BF16), preferred_element_type=F32)

    def softmax_pv(blk, s_ref, diag):
        off = blk * tk
        for c in range(n_chunks):
            cs = slice(c * cw, (c + 1) * cw)
            nk, masked = tk, False
            if diag is not None:
                q0 = (c * cw) % tq
                nk = min(tk, q0 + cw - diag * tk)
                if nk <= 0:
                    continue
                masked = (diag * tk + nk - 1) // CHUNK > q0 // CHUNK
            s = s_ref[c, 0:nk, :]
            if masked:
                kchunk = (lax.broadcasted_iota(jnp.int32, (nk, cw), 0) + diag * tk) // CHUNK
                qchunk = (lax.broadcasted_iota(jnp.int32, (nk, cw), 1) + q0) // CHUNK
                s = jnp.where(kchunk <= qchunk, s, NEG_INF)
            m_sc[:, cs], acc_sc[:, cs] = online_update(s, vt_ref[:, off:off + nk], m_sc[:, cs], acc_sc[:, cs])

    def run(tile, r0):
        qt = q_ref[r0:r0 + tq, :].astype(F32).T
        row = lax.broadcasted_iota(jnp.int32, qt.shape, 0)
        qs = jnp.concatenate([jnp.where(row < QK_DIM, qt, 0.0),
                              jnp.where(row < QK_DIM, 0.0, qt)], axis=1).astype(BF16)
        per_tile = tq // tk
        blocks = [(b, None) for b in range(tile * per_tile)] + [(tile * per_tile + d, d) for d in range(per_tile)]
        bufs = (s0_sc, s1_sc)
        scores(blocks[0][0], bufs[0], qs)
        s = jnp.dot(km_ref[...], qs, preferred_element_type=F32)
        m_sc[...], acc_sc[...] = online_update(s, vmt_ref[...], jnp.full(m_sc.shape, NEG_INF, F32),
                                               jnp.zeros(acc_sc.shape, F32))
        for t, (blk, diag) in enumerate(blocks):
            if t + 1 < len(blocks):
                scores(blocks[t + 1][0], bufs[(t + 1) % 2], qs)
            softmax_pv(blk, bufs[t % 2], diag)

        lam = (jnp.exp(jnp.sum(lq1_ref[...] * lk1_ref[...], axis=-1, keepdims=True))
               - jnp.exp(jnp.sum(lq2_ref[...] * lk2_ref[...], axis=-1, keepdims=True)) + LAM_INIT)
        acc = acc_sc[...]
        o = acc[:HEAD_DIM] * (1.0 / acc[HEAD_DIM:HEAD_DIM + 1])
        od = o[:, :tq] - lam * o[:, tq:]
        ms = jnp.mean(od * od, axis=0, keepdims=True)
        on = od * lax.rsqrt(ms + EPS) * sg_ref[...] * (1.0 - LAM_INIT)
        o_ref[r0:r0 + tq, :] = on.T.astype(BF16)

    def step(first_tile):
        for w32, w16 in zip(w32_refs, w16_refs):
            w16[...] = w32[...].astype(BF16)
        for t in range(tiles_per_step):
            run(first_tile + t, t * tq)

    for n in range(SEQ // (tq * tiles_per_step)):
        pl.when(i == n)(functools.partial(step, n * tiles_per_step))


def _attention(q, k, vt, km, vmt, lq1, lk1, lq2, lk2, sg, cast_ws, *, batch, tq, tk, cw):
    assert tq % tk == 0 and tk % cw == 0 and cw % CHUNK == 0
    tiles_per_step = 2
    nq = SEQ // (tq * tiles_per_step)
    steps = batch * N_HEADS * nq
    lspec = pl.BlockSpec((1, QK_DIM), lambda b, h, i: (0, 0))

    def slab(w):
        rep = next(r for r in (1, 2, 4, 8) if w.shape[0] * r % (BF16_ROWS * steps) == 0)
        return pl.BlockSpec((w.shape[0] * rep // steps, w.shape[1]),
                            lambda b, h, i: (((b * N_HEADS + h) * nq + i) // rep, 0))

    outs = pl.pallas_call(
        functools.partial(_attn_kernel, tq=tq, tk=tk, cw=cw, n_cast=len(cast_ws), tiles_per_step=tiles_per_step),
        grid=(batch, N_HEADS, nq),
        in_specs=[
            pl.BlockSpec((tq * tiles_per_step, HEAD_DIM), lambda b, h, i: (b * nq + i, h)),
            pl.BlockSpec((SEQ, HEAD_DIM), lambda b, h, i: (b, h)),
            pl.BlockSpec((HEAD_DIM, SEQ), lambda b, h, i: (h, b)),
            pl.BlockSpec((N_META, HEAD_DIM), lambda b, h, i: (0, h)),
            pl.BlockSpec((HEAD_DIM, N_META), lambda b, h, i: (h, 0)),
            lspec, lspec, lspec, lspec,
            pl.BlockSpec((HEAD_DIM, 1), lambda b, h, i: (0, 0)),
        ] + [slab(w) for w in cast_ws],
        out_specs=[pl.BlockSpec((tq * tiles_per_step, HEAD_DIM), lambda b, h, i: (b * nq + i, h))]
        + [slab(w) for w in cast_ws],
        out_shape=[jax.ShapeDtypeStruct((batch * SEQ, ATTN_WIDTH), BF16)]
        + [jax.ShapeDtypeStruct(w.shape, BF16) for w in cast_ws],
        scratch_shapes=[
            pltpu.VMEM((1, 2 * tq), F32),
            pltpu.VMEM((HEAD_DIM + DENOM_ROWS, 2 * tq), F32),
            pltpu.VMEM((2 * tq // cw, tk, cw), F32),
            pltpu.VMEM((2 * tq // cw, tk, cw), F32),
        ],
        compiler_params=_params("parallel", "parallel", "arbitrary"),
        name="diff_attn",
    )(q, k, vt, km, vmt, lq1, lk1, lq2, lk2, sg, *cast_ws)
    return outs[0], outs[1:]


def _conv_kernel(u_ref, prev_ref, halo0_ref, w_ref, b_ref, y_ref, win_sc, shift_sc, *, tl, rc, rb):
    i = pl.program_id(1)

    @pl.when(i == 0)
    def _():
        win_sc[0:HALO, :] = halo0_ref[...]

    @pl.when(i > 0)
    def _():
        win_sc[0:HALO, :] = prev_ref[...]

    win_sc[HALO:HALO + tl, :] = u_ref[...]

    nsh = HALO + tl - SUBLANES
    for s in range(1, SUBLANES):
        shift_sc[s - 1] = win_sc[s:s + nsh, :]

    first = HALO - (CONV_KERNEL - 1)
    @pl.loop(0, CONV_WIDTH // LANES)
    def _(c):
        cs = pl.ds(pl.multiple_of(c * LANES, LANES), LANES)
        for r0 in range(0, tl // rc, rb):
            accs = [jnp.broadcast_to(b_ref[:, cs], (rc, LANES))] * rb
            for t in range(CONV_KERNEL):
                wt = w_ref[t:t + 1, cs]
                for k in range(rb):
                    lo = (r0 + k) * rc + first + t
                    s, al = lo % SUBLANES, lo - lo % SUBLANES
                    tap = win_sc[al:al + rc, cs] if s == 0 else shift_sc[s - 1, al:al + rc, cs]
                    accs[k] = accs[k] + wt * tap
            for k in range(rb):
                y_ref[(r0 + k) * rc:(r0 + k + 1) * rc, cs] = accs[k]


def _conv(u, halo0, w, b, *, batch, tl, rc, rb):
    nt = SEQ // tl
    hb = tl // HALO
    return pl.pallas_call(
        functools.partial(_conv_kernel, tl=tl, rc=rc, rb=rb),
        grid=(batch, nt),
        in_specs=[
            pl.BlockSpec((tl, CONV_WIDTH), lambda bb, i: (bb * nt + i, 0)),
            pl.BlockSpec((HALO, CONV_WIDTH), lambda bb, i: (jnp.maximum((bb * nt + i) * hb - 1, 0), 0)),
            pl.BlockSpec((HALO, CONV_WIDTH), lambda bb, i: (0, 0)),
            pl.BlockSpec((CONV_KERNEL, CONV_WIDTH), lambda bb, i: (0, 0)),
            pl.BlockSpec((1, CONV_WIDTH), lambda bb, i: (0, 0)),
        ],
        out_specs=pl.BlockSpec((tl, CONV_WIDTH), lambda bb, i: (bb * nt + i, 0)),
        out_shape=jax.ShapeDtypeStruct((batch * SEQ, CONV_WIDTH), F32),
        scratch_shapes=[
            pltpu.VMEM((HALO + tl, CONV_WIDTH), F32),
            pltpu.VMEM((SUBLANES - 1, HALO + tl - SUBLANES, CONV_WIDTH), F32),
        ],
        compiler_params=_params("parallel", "arbitrary"),
        name="conv",
    )(u, u, halo0, w, b)


def _outproj_kernel(x_ref, a_ref, y_ref, cng_ref, wa_ref, wc_ref, o_ref):
    att = jnp.dot(a_ref[...], wa_ref[...], preferred_element_type=F32)
    yn = _rms(y_ref[...], cng_ref[...])
    oc = (yn * _sigmoid(yn)).astype(BF16)
    o_ref[...] = x_ref[...] + att + jnp.dot(oc, wc_ref[...], preferred_element_type=F32)


def _out_proj(x, oa, y, cng, w_out, *, tm):
    m = x.shape[0]
    return pl.pallas_call(
        _outproj_kernel,
        grid=(m // tm,),
        in_specs=[
            pl.BlockSpec((tm, D_MODEL), lambda i: (i, 0)),
            pl.BlockSpec((tm, ATTN_WIDTH), lambda i: (i, 0)),
            pl.BlockSpec((tm, CONV_WIDTH), lambda i: (i, 0)),
            pl.BlockSpec((1, CONV_WIDTH), lambda i: (0, 0)),
            pl.BlockSpec((ATTN_WIDTH, D_MODEL), lambda i: (0, 0)),
            pl.BlockSpec((CONV_WIDTH, D_MODEL), lambda i: (1, 0)),
        ],
        out_specs=pl.BlockSpec((tm, D_MODEL), lambda i: (i, 0)),
        out_shape=jax.ShapeDtypeStruct((m, D_MODEL), F32),
        compiler_params=_params("parallel"),
        name="out_proj",
    )(x, oa, y, cng, w_out, w_out)


def kernel(x, meta_tokens, ffn1_norm_g, ffn1_w_gate, ffn1_w_up, ffn1_w_down, mix_norm_g, w_in, q_norm_g,
           k_norm_g, lambda_q1, lambda_k1, lambda_q2, lambda_k2, attn_subln_g, conv_w, conv_b, conv_norm_g,
           w_out, ffn2_norm_g, ffn2_w_gate, ffn2_w_up, ffn2_w_down, final_norm_g):
    batch = x.shape[0]
    xr = x.reshape(batch * SEQ, D_MODEL)

    w1g, w1u, w1d = ffn1_w_gate[0], ffn1_w_up[0], ffn1_w_down[0]
    w2g, w2u, w2d = ffn2_w_gate[0], ffn2_w_up[0], ffn2_w_down[0]
    win, wout = w_in[0], w_out[0]
    qg = jnp.tile(q_norm_g[0] * (QK_DIM ** -0.5 * LOG2E), 2)[None, :]
    kg = jnp.tile(k_norm_g[0], 2)[None, :]
    sg = attn_subln_g[0][:, None]
    cw = conv_w[0].reshape(CONV_KERNEL, CONV_WIDTH)

    m1, (w1g, w1u, w1d) = _ffn(meta_tokens, ffn1_norm_g, w1g, w1u, w1d, final_norm_g,
                               tm=N_META, tf=META_FFN_TF, final_norm=False, emit_w16=True)
    (_, km, vmt, um), win16 = _in_proj(m1, mix_norm_g, win, qg, kg, tm=N_META, tn=PROJ_TN, emit_w16=True)
    halo0 = jnp.concatenate([jnp.zeros((HALO - N_META, CONV_WIDTH), F32), um], axis=0)

    x1 = _ffn(xr, ffn1_norm_g, w1g, w1u, w1d, final_norm_g, tm=FFN_TM, tf=FFN_TF, final_norm=False)
    q, k, vt, u = _in_proj(x1, mix_norm_g, win16, qg, kg, tm=PROJ_TM, tn=PROJ_TN)
    oa, (w2g, w2u, w2d, wout) = _attention(q, k, vt, km, vmt, lambda_q1, lambda_k1, lambda_q2, lambda_k2, sg,
                                           [w2g, w2u, w2d, wout], batch=batch,
                                           tq=ATTN_TQ, tk=ATTN_TK, cw=ATTN_CW)
    yc = _conv(u, halo0, cw, conv_b, batch=batch, tl=CONV_TL, rc=CONV_RC, rb=CONV_RB)
    x2 = _out_proj(x1, oa, yc, conv_norm_g, wout, tm=OUT_TM)
    y = _ffn(x2, ffn2_norm_g, w2g, w2u, w2d, final_norm_g, tm=FFN_TM, tf=FFN_TF, final_norm=True)
    return y.reshape(batch, SEQ, D_MODEL)
```

```python
import functools
import math

import jax
import jax.numpy as jnp
from jax import lax
from jax.experimental import pallas as pl
from jax.experimental.pallas import tpu as pltpu

F32 = jnp.float32
BF16 = jnp.bfloat16

D_MODEL = 2048
SEQ = 4096
N_META = 16
CHUNK = 64
ATTN_WIDTH = 1024
CONV_WIDTH = 1024
HEAD_DIM = 128
N_HEADS = ATTN_WIDTH // HEAD_DIM
QK_DIM = HEAD_DIM // 2
CONV_KERNEL = 31
D_FF = 5632
EPS = 1e-6
NEG_INF = -1e30
LAM_INIT = 0.8 - 0.6 * math.exp(-0.3 * 1)

LOG2E = math.log2(math.e)

LANES = 128
SUBLANES = 8
BF16_ROWS = 16
DENOM_ROWS = 16
HALO = 32
VMEM_LIMIT = 56 * 1024 * 1024

FFN_TM, FFN_TF = 1024, 512
META_FFN_TF = 512
PROJ_TM, PROJ_TN = 1024, 256
ATTN_TQ, ATTN_TK, ATTN_CW = 1024, 512, 256
ATTN_TILES_PER_STEP = 2
CONV_TL, CONV_RC, CONV_RB = 512, 32, 8
OUT_TM = 512


def _params(*sem):
    return pltpu.CompilerParams(dimension_semantics=sem, vmem_limit_bytes=VMEM_LIMIT)


def _rms(x, g):
    ms = jnp.mean(x * x, axis=-1, keepdims=True)
    return x * lax.rsqrt(ms + EPS) * g


def _sigmoid(x):
    return 1.0 / (1.0 + jnp.exp(-x))


def _ffn_kernel(x_ref, g_ref, wg_ref, wu_ref, wd_ref, fg_ref, o_ref, *rest, final_norm, emit_w16):
    h_ref = rest[-1]
    j = pl.program_id(1)

    @pl.when(j == 0)
    def _():
        x = x_ref[...]
        h_ref[...] = _rms(x, g_ref[...]).astype(BF16)
        o_ref[...] = x

    wg, wu, wd = (w_ref[...].astype(BF16) for w_ref in (wg_ref, wu_ref, wd_ref))
    if emit_w16:
        for w16_ref, w in zip(rest[:3], (wg, wu, wd)):
            w16_ref[...] = w
    h = h_ref[...]
    gate = jnp.dot(h, wg, preferred_element_type=F32)
    up = jnp.dot(h, wu, preferred_element_type=F32)
    a = (gate * _sigmoid(gate)) * (up * 0.5)
    o_ref[...] += jnp.dot(a.astype(BF16), wd, preferred_element_type=F32)

    if final_norm:
        @pl.when(j == pl.num_programs(1) - 1)
        def _():
            o_ref[...] = _rms(o_ref[...], fg_ref[...])


def _ffn(x, g, wg, wu, wd, fg, *, tm, tf, final_norm, emit_w16=False):
    m = x.shape[0]
    grid = (m // tm, D_FF // tf)
    assert not emit_w16 or grid[0] == 1
    wspecs = [
        pl.BlockSpec((D_MODEL, tf), lambda i, j: (0, j)),
        pl.BlockSpec((D_MODEL, tf), lambda i, j: (0, j)),
        pl.BlockSpec((tf, D_MODEL), lambda i, j: (j, 0)),
    ]
    out_specs = [pl.BlockSpec((tm, D_MODEL), lambda i, j: (i, 0))]
    out_shape = [jax.ShapeDtypeStruct((m, D_MODEL), F32)]
    if emit_w16:
        out_specs += wspecs
        out_shape += [jax.ShapeDtypeStruct(w.shape, BF16) for w in (wg, wu, wd)]
    outs = pl.pallas_call(
        functools.partial(_ffn_kernel, final_norm=final_norm, emit_w16=emit_w16),
        grid=grid,
        in_specs=[
            pl.BlockSpec((tm, D_MODEL), lambda i, j: (i, 0)),
            pl.BlockSpec((1, D_MODEL), lambda i, j: (0, 0)),
            *wspecs,
            pl.BlockSpec((1, D_MODEL), lambda i, j: (0, 0)),
        ],
        out_specs=out_specs,
        out_shape=out_shape,
        scratch_shapes=[pltpu.VMEM((tm, D_MODEL), BF16)],
        compiler_params=_params("parallel", "arbitrary"),
        name="ffn_final" if final_norm else "ffn",
    )(x, g, wg, wu, wd, fg)
    return (outs[0], outs[1:]) if emit_w16 else outs[0]


def _qk_norm(x, g):
    outs = []
    for hh in range(x.shape[1] // LANES):
        xh = x[:, hh * LANES:(hh + 1) * LANES]
        lo = lax.broadcasted_iota(jnp.int32, xh.shape, 1) < QK_DIM
        x2 = xh * xh
        s_lo = jnp.sum(jnp.where(lo, x2, 0.0), axis=-1, keepdims=True)
        s_hi = jnp.sum(jnp.where(lo, 0.0, x2), axis=-1, keepdims=True)
        ms = jnp.where(lo, s_lo, s_hi) * (1.0 / QK_DIM)
        outs.append(xh * lax.rsqrt(ms + EPS) * g)
    return outs[0] if len(outs) == 1 else jnp.concatenate(outs, axis=1)


def _inproj_kernel(x_ref, g_ref, wq_ref, wk_ref, wv_ref, wa_ref, wc_ref, qg_ref, kg_ref,
                   q_ref, k_ref, vt_ref, u_ref, *rest, emit_w16):
    h_ref = rest[-1]
    j = pl.program_id(1)

    @pl.when(j == 0)
    def _():
        h_ref[...] = _rms(x_ref[...], g_ref[...]).astype(BF16)

    ws = [w_ref[...].astype(BF16) for w_ref in (wq_ref, wk_ref, wv_ref, wa_ref, wc_ref)]
    if emit_w16:
        for w16_ref, w in zip(rest[:5], ws):
            w16_ref[...] = w
    h = h_ref[...]
    q, k, v, ca, cg = (jnp.dot(h, w, preferred_element_type=F32) for w in ws)
    q_ref[...] = _qk_norm(q, qg_ref[...]).astype(BF16)
    k_ref[...] = _qk_norm(k, kg_ref[...]).astype(BF16)
    vt_ref[...] = v.T.astype(BF16)
    u_ref[...] = ca * _sigmoid(cg)


def _in_proj(x, g, ws, qg, kg, *, tm, tn, emit_w16=False):
    m = x.shape[0]
    nb = ATTN_WIDTH // tn
    assert not emit_w16 or m == tm
    if isinstance(ws, (tuple, list)):
        wspecs = [pl.BlockSpec((D_MODEL, tn), lambda i, j: (0, j))] * 5
    else:
        wspecs = [pl.BlockSpec((D_MODEL, tn), lambda i, j, c=c: (0, c * nb + j)) for c in range(5)]
        ws = [ws] * 5
    out_specs = [
        pl.BlockSpec((tm, tn), lambda i, j: (i, j)),
        pl.BlockSpec((tm, tn), lambda i, j: (i, j)),
        pl.BlockSpec((tn, tm), lambda i, j: (j, i)),
        pl.BlockSpec((tm, tn), lambda i, j: (i, j)),
    ]
    out_shape = [
        jax.ShapeDtypeStruct((m, ATTN_WIDTH), BF16),
        jax.ShapeDtypeStruct((m, ATTN_WIDTH), BF16),
        jax.ShapeDtypeStruct((ATTN_WIDTH, m), BF16),
        jax.ShapeDtypeStruct((m, CONV_WIDTH), F32),
    ]
    if emit_w16:
        out_specs += [pl.BlockSpec((D_MODEL, tn), lambda i, j: (0, j))] * 5
        out_shape += [jax.ShapeDtypeStruct((D_MODEL, ATTN_WIDTH), BF16)] * 5
    outs = pl.pallas_call(
        functools.partial(_inproj_kernel, emit_w16=emit_w16),
        grid=(m // tm, nb),
        in_specs=[
            pl.BlockSpec((tm, D_MODEL), lambda i, j: (i, 0)),
            pl.BlockSpec((1, D_MODEL), lambda i, j: (0, 0)),
            *wspecs,
            pl.BlockSpec((1, LANES), lambda i, j: (0, 0)),
            pl.BlockSpec((1, LANES), lambda i, j: (0, 0)),
        ],
        out_specs=out_specs,
        out_shape=out_shape,
        scratch_shapes=[pltpu.VMEM((tm, D_MODEL), BF16)],
        compiler_params=_params("parallel", "arbitrary"),
        name="in_proj",
    )(x, g, *ws, qg, kg)
    return (outs[:4], outs[4:]) if emit_w16 else outs


def _attn_kernel(q_ref, k_ref, vt_ref, km_ref, vmt_ref, lq1_ref, lk1_ref, lq2_ref, lk2_ref, sg_ref,
                 *rest, tq, tk, cw, n_cast, tiles_per_step):
    w32_refs, o_ref, w16_refs = rest[:n_cast], rest[n_cast], rest[n_cast + 1:2 * n_cast + 1]
    m_sc, acc_sc, s0_sc, s1_sc = rest[2 * n_cast + 1:]
    n_chunks = 2 * tq // cw
    i = pl.program_id(2)

    def scores(blk, s_ref, qs):
        kb = k_ref[blk * tk:(blk + 1) * tk, :]
        for c in range(n_chunks):
            s_ref[c] = jnp.dot(kb, qs[:, c * cw:(c + 1) * cw], preferred_element_type=F32)

    def with_ones(vt):
        return jnp.concatenate([vt, jnp.ones((DENOM_ROWS, vt.shape[1]), BF16)], axis=0)

    def online_update(s, vt, m_old, acc_old):
        m_new = jnp.maximum(m_old, jnp.max(s, axis=0, keepdims=True))
        alpha = jnp.exp2(m_old - m_new)
        p = jnp.exp2(s - m_new)
        return m_new, alpha * acc_old + jnp.dot(with_ones(vt), p.astype(BF16), preferred_element_type=F32)

    def softmax_pv(blk, s_ref, diag):
        off = blk * tk
        for c in range(n_chunks):
            cs = slice(c * cw, (c + 1) * cw)
            nk, masked = tk, False
            if diag is not None:
                q0 = (c * cw) % tq
                nk = min(tk, q0 + cw - diag * tk)
                if nk <= 0:
                    continue
                masked = (diag * tk + nk - 1) // CHUNK > q0 // CHUNK
            s = s_ref[c, 0:nk, :]
            if masked:
                kchunk = (lax.broadcasted_iota(jnp.int32, (nk, cw), 0) + diag * tk) // CHUNK
                qchunk = (lax.broadcasted_iota(jnp.int32, (nk, cw), 1) + q0) // CHUNK
                s = jnp.where(kchunk <= qchunk, s, NEG_INF)
            m_sc[:, cs], acc_sc[:, cs] = online_update(s, vt_ref[:, off:off + nk], m_sc[:, cs], acc_sc[:, cs])

    def run(tile, r0):
        qt = q_ref[r0:r0 + tq, :].astype(F32).T
        row = lax.broadcasted_iota(jnp.int32, qt.shape, 0)
        qs = jnp.concatenate([jnp.where(row < QK_DIM, qt, 0.0),
                              jnp.where(row < QK_DIM, 0.0, qt)], axis=1).astype(BF16)
        per_tile = tq // tk
        blocks = [(b, None) for b in range(tile * per_tile)] + [(tile * per_tile + d, d) for d in range(per_tile)]
        bufs = (s0_sc, s1_sc)
        scores(blocks[0][0], bufs[0], qs)
        s = jnp.dot(km_ref[...], qs, preferred_element_type=F32)
        m_sc[...], acc_sc[...] = online_update(s, vmt_ref[...], jnp.full(m_sc.shape, NEG_INF, F32),
                                               jnp.zeros(acc_sc.shape, F32))
        for t, (blk, diag) in enumerate(blocks):
            if t + 1 < len(blocks):
                scores(blocks[t + 1][0], bufs[(t + 1) % 2], qs)
            softmax_pv(blk, bufs[t % 2], diag)

        lam = (jnp.exp(jnp.sum(lq1_ref[...] * lk1_ref[...], axis=-1, keepdims=True))
               - jnp.exp(jnp.sum(lq2_ref[...] * lk2_ref[...], axis=-1, keepdims=True)) + LAM_INIT)
        acc = acc_sc[...]
        o = acc[:HEAD_DIM] * (1.0 / acc[HEAD_DIM:HEAD_DIM + 1])
        od = o[:, :tq] - lam * o[:, tq:]
        ms = jnp.mean(od * od, axis=0, keepdims=True)
        on = od * lax.rsqrt(ms + EPS) * sg_ref[...] * (1.0 - LAM_INIT)
        o_ref[r0:r0 + tq, :] = on.T.astype(BF16)

    def step(first_tile):
        for w32, w16 in zip(w32_refs, w16_refs):
            w16[...] = w32[...].astype(BF16)
        for t in range(tiles_per_step):
            run(first_tile + t, t * tq)

    for n in range(SEQ // (tq * tiles_per_step)):
        pl.when(i == n)(functools.partial(step, n * tiles_per_step))


def _attention(q, k, vt, km, vmt, lq1, lk1, lq2, lk2, sg, cast_ws, *, batch, tq, tk, cw, tiles_per_step):
    assert tq % tk == 0 and tk % cw == 0 and cw % CHUNK == 0
    nq = SEQ // (tq * tiles_per_step)
    steps = batch * N_HEADS * nq
    lspec = pl.BlockSpec((1, QK_DIM), lambda b, h, i: (0, 0))

    def slab(w):
        rep = next(r for r in (1, 2, 4, 8) if w.shape[0] * r % (BF16_ROWS * steps) == 0)
        return pl.BlockSpec((w.shape[0] * rep // steps, w.shape[1]),
                            lambda b, h, i: (((b * N_HEADS + h) * nq + i) // rep, 0))

    outs = pl.pallas_call(
        functools.partial(_attn_kernel, tq=tq, tk=tk, cw=cw, n_cast=len(cast_ws), tiles_per_step=tiles_per_step),
        grid=(batch, N_HEADS, nq),
        in_specs=[
            pl.BlockSpec((tq * tiles_per_step, HEAD_DIM), lambda b, h, i: (b * nq + i, h)),
            pl.BlockSpec((SEQ, HEAD_DIM), lambda b, h, i: (b, h)),
            pl.BlockSpec((HEAD_DIM, SEQ), lambda b, h, i: (h, b)),
            pl.BlockSpec((N_META, HEAD_DIM), lambda b, h, i: (0, h)),
            pl.BlockSpec((HEAD_DIM, N_META), lambda b, h, i: (h, 0)),
            lspec, lspec, lspec, lspec,
            pl.BlockSpec((HEAD_DIM, 1), lambda b, h, i: (0, 0)),
        ] + [slab(w) for w in cast_ws],
        out_specs=[pl.BlockSpec((tq * tiles_per_step, HEAD_DIM), lambda b, h, i: (b * nq + i, h))]
        + [slab(w) for w in cast_ws],
        out_shape=[jax.ShapeDtypeStruct((batch * SEQ, ATTN_WIDTH), BF16)]
        + [jax.ShapeDtypeStruct(w.shape, BF16) for w in cast_ws],
        scratch_shapes=[
            pltpu.VMEM((1, 2 * tq), F32),
            pltpu.VMEM((HEAD_DIM + DENOM_ROWS, 2 * tq), F32),
            pltpu.VMEM((2 * tq // cw, tk, cw), F32),
            pltpu.VMEM((2 * tq // cw, tk, cw), F32),
        ],
        compiler_params=_params("parallel", "parallel", "arbitrary"),
        name="diff_attn",
    )(q, k, vt, km, vmt, lq1, lk1, lq2, lk2, sg, *cast_ws)
    return outs[0], outs[1:]


def _conv_kernel(u_ref, prev_ref, halo0_ref, w_ref, b_ref, y_ref, win_sc, shift_sc, *, tl, rc, rb):
    i = pl.program_id(1)

    @pl.when(i == 0)
    def _():
        win_sc[0:HALO, :] = halo0_ref[...]

    @pl.when(i > 0)
    def _():
        win_sc[0:HALO, :] = prev_ref[...]

    win_sc[HALO:HALO + tl, :] = u_ref[...]

    nsh = HALO + tl - SUBLANES
    for s in range(1, SUBLANES):
        shift_sc[s - 1] = win_sc[s:s + nsh, :]

    first = HALO - (CONV_KERNEL - 1)
    @pl.loop(0, CONV_WIDTH // LANES)
    def _(c):
        cs = pl.ds(pl.multiple_of(c * LANES, LANES), LANES)
        for r0 in range(0, tl // rc, rb):
            accs = [jnp.broadcast_to(b_ref[:, cs], (rc, LANES))] * rb
            for t in range(CONV_KERNEL):
                wt = w_ref[t:t + 1, cs]
                for k in range(rb):
                    lo = (r0 + k) * rc + first + t
                    s, al = lo % SUBLANES, lo - lo % SUBLANES
                    tap = win_sc[al:al + rc, cs] if s == 0 else shift_sc[s - 1, al:al + rc, cs]
                    accs[k] = accs[k] + wt * tap
            for k in range(rb):
                y_ref[(r0 + k) * rc:(r0 + k + 1) * rc, cs] = accs[k]


def _conv(u, halo0, w, b, *, batch, tl, rc, rb):
    nt = SEQ // tl
    hb = tl // HALO
    return pl.pallas_call(
        functools.partial(_conv_kernel, tl=tl, rc=rc, rb=rb),
        grid=(batch, nt),
        in_specs=[
            pl.BlockSpec((tl, CONV_WIDTH), lambda bb, i: (bb * nt + i, 0)),
            pl.BlockSpec((HALO, CONV_WIDTH), lambda bb, i: (jnp.maximum((bb * nt + i) * hb - 1, 0), 0)),
            pl.BlockSpec((HALO, CONV_WIDTH), lambda bb, i: (0, 0)),
            pl.BlockSpec((CONV_KERNEL, CONV_WIDTH), lambda bb, i: (0, 0)),
            pl.BlockSpec((1, CONV_WIDTH), lambda bb, i: (0, 0)),
        ],
        out_specs=pl.BlockSpec((tl, CONV_WIDTH), lambda bb, i: (bb * nt + i, 0)),
        out_shape=jax.ShapeDtypeStruct((batch * SEQ, CONV_WIDTH), F32),
        scratch_shapes=[
            pltpu.VMEM((HALO + tl, CONV_WIDTH), F32),
            pltpu.VMEM((SUBLANES - 1, HALO + tl - SUBLANES, CONV_WIDTH), F32),
        ],
        compiler_params=_params("parallel", "arbitrary"),
        name="conv",
    )(u, u, halo0, w, b)


def _outproj_kernel(x_ref, a_ref, y_ref, cng_ref, wa_ref, wc_ref, o_ref):
    att = jnp.dot(a_ref[...], wa_ref[...], preferred_element_type=F32)
    yn = _rms(y_ref[...], cng_ref[...])
    oc = (yn * _sigmoid(yn)).astype(BF16)
    o_ref[...] = x_ref[...] + att + jnp.dot(oc, wc_ref[...], preferred_element_type=F32)


def _out_proj(x, oa, y, cng, w_out, *, tm):
    m = x.shape[0]
    return pl.pallas_call(
        _outproj_kernel,
        grid=(m // tm,),
        in_specs=[
            pl.BlockSpec((tm, D_MODEL), lambda i: (i, 0)),
            pl.BlockSpec((tm, ATTN_WIDTH), lambda i: (i, 0)),
            pl.BlockSpec((tm, CONV_WIDTH), lambda i: (i, 0)),
            pl.BlockSpec((1, CONV_WIDTH), lambda i: (0, 0)),
            pl.BlockSpec((ATTN_WIDTH, D_MODEL), lambda i: (0, 0)),
            pl.BlockSpec((CONV_WIDTH, D_MODEL), lambda i: (1, 0)),
        ],
        out_specs=pl.BlockSpec((tm, D_MODEL), lambda i: (i, 0)),
        out_shape=jax.ShapeDtypeStruct((m, D_MODEL), F32),
        compiler_params=_params("parallel"),
        name="out_proj",
    )(x, oa, y, cng, w_out, w_out)


def kernel(x, meta_tokens, ffn1_norm_g, ffn1_w_gate, ffn1_w_up, ffn1_w_down, mix_norm_g, w_in, q_norm_g,
           k_norm_g, lambda_q1, lambda_k1, lambda_q2, lambda_k2, attn_subln_g, conv_w, conv_b, conv_norm_g,
           w_out, ffn2_norm_g, ffn2_w_gate, ffn2_w_up, ffn2_w_down, final_norm_g):
    batch = x.shape[0]
    xr = x.reshape(batch * SEQ, D_MODEL)

    w1g, w1u, w1d = ffn1_w_gate[0], ffn1_w_up[0], ffn1_w_down[0]
    w2g, w2u, w2d = ffn2_w_gate[0], ffn2_w_up[0], ffn2_w_down[0]
    win, wout = w_in[0], w_out[0]
    qg = jnp.tile(q_norm_g[0] * (QK_DIM ** -0.5 * LOG2E), 2)[None, :]
    kg = jnp.tile(k_norm_g[0], 2)[None, :]
    sg = attn_subln_g[0][:, None]
    cw = conv_w[0].reshape(CONV_KERNEL, CONV_WIDTH)

    m1, (w1g, w1u, w1d) = _ffn(meta_tokens, ffn1_norm_g, w1g, w1u, w1d, final_norm_g,
                               tm=N_META, tf=META_FFN_TF, final_norm=False, emit_w16=True)
    (_, km, vmt, um), win16 = _in_proj(m1, mix_norm_g, win, qg, kg, tm=N_META, tn=PROJ_TN, emit_w16=True)
    halo0 = jnp.concatenate([jnp.zeros((HALO - N_META, CONV_WIDTH), F32), um], axis=0)

    x1 = _ffn(xr, ffn1_norm_g, w1g, w1u, w1d, final_norm_g, tm=FFN_TM, tf=FFN_TF, final_norm=False)
    q, k, vt, u = _in_proj(x1, mix_norm_g, win16, qg, kg, tm=PROJ_TM, tn=PROJ_TN)
    oa, (w2g, w2u, w2d, wout) = _attention(q, k, vt, km, vmt, lambda_q1, lambda_k1, lambda_q2, lambda_k2, sg,
                                           [w2g, w2u, w2d, wout], batch=batch,
                                           tq=ATTN_TQ, tk=ATTN_TK, cw=ATTN_CW,
                                           tiles_per_step=ATTN_TILES_PER_STEP)
    yc = _conv(u, halo0, cw, conv_b, batch=batch, tl=CONV_TL, rc=CONV_RC, rb=CONV_RB)
    x2 = _out_proj(x1, oa, yc, conv_norm_g, wout, tm=OUT_TM)
    y = _ffn(x2, ffn2_norm_g, w2g, w2u, w2d, final_norm_g, tm=FFN_TM, tf=FFN_TF, final_norm=True)
    return y.reshape(batch, SEQ, D_MODEL)
```

```python
import functools
import math

import jax
import jax.numpy as jnp
from jax import lax
from jax.experimental import pallas as pl
from jax.experimental.pallas import tpu as pltpu

F32 = jnp.float32
BF16 = jnp.bfloat16

D_MODEL = 2048
SEQ = 4096
N_META = 16
CHUNK = 64
ATTN_WIDTH = 1024
CONV_WIDTH = 1024
HEAD_DIM = 128
N_HEADS = ATTN_WIDTH // HEAD_DIM
QK_DIM = HEAD_DIM // 2
CONV_KERNEL = 31
D_FF = 5632
EPS = 1e-6
NEG_INF = -1e30
LAM_INIT = 0.8 - 0.6 * math.exp(-0.3 * 1)

LOG2E = math.log2(math.e)

LANES = 128
SUBLANES = 8
BF16_ROWS = 16
DENOM_ROWS = 16
HALO = 32
VMEM_LIMIT = 56 * 1024 * 1024

FFN_TM, FFN_TF = 1024, 512
META_FFN_TF = 512
PROJ_TM, PROJ_TN = 1024, 256
ATTN_TQ, ATTN_TK, ATTN_CW = 1024, 512, 256
ATTN_TILES_PER_STEP = 2
CONV_TL, CONV_RC, CONV_RB = 512, 32, 8
OUT_TM = 512


def _params(*sem):
    return pltpu.CompilerParams(dimension_semantics=sem, vmem_limit_bytes=VMEM_LIMIT)


def _rms(x, g):
    ms = jnp.mean(x * x, axis=-1, keepdims=True)
    return x * lax.rsqrt(ms + EPS) * g


def _sigmoid(x):
    return 1.0 / (1.0 + jnp.exp(-x))


def _ffn_kernel(x_ref, g_ref, wg_ref, wu_ref, wd_ref, fg_ref, o_ref, *rest, final_norm, emit_w16):
    h_ref = rest[-1]
    j = pl.program_id(1)

    @pl.when(j == 0)
    def _():
        x = x_ref[...]
        h_ref[...] = _rms(x, g_ref[...]).astype(BF16)
        o_ref[...] = x

    wg, wu, wd = (w_ref[...].astype(BF16) for w_ref in (wg_ref, wu_ref, wd_ref))
    if emit_w16:
        for w16_ref, w in zip(rest[:3], (wg, wu, wd)):
            w16_ref[...] = w
    h = h_ref[...]
    gate = jnp.dot(h, wg, preferred_element_type=F32)
    up = jnp.dot(h, wu, preferred_element_type=F32)
    a = (gate * _sigmoid(gate)) * (up * 0.5)
    o_ref[...] += jnp.dot(a.astype(BF16), wd, preferred_element_type=F32)

    if final_norm:
        @pl.when(j == pl.num_programs(1) - 1)
        def _():
            o_ref[...] = _rms(o_ref[...], fg_ref[...])


def _ffn(x, g, wg, wu, wd, fg, *, tm, tf, final_norm, emit_w16=False):
    m = x.shape[0]
    grid = (m // tm, D_FF // tf)
    assert not emit_w16 or grid[0] == 1
    wspecs = [
        pl.BlockSpec((D_MODEL, tf), lambda i, j: (0, j)),
        pl.BlockSpec((D_MODEL, tf), lambda i, j: (0, j)),
        pl.BlockSpec((tf, D_MODEL), lambda i, j: (j, 0)),
    ]
    out_specs = [pl.BlockSpec((tm, D_MODEL), lambda i, j: (i, 0))]
    out_shape = [jax.ShapeDtypeStruct((m, D_MODEL), F32)]
    if emit_w16:
        out_specs += wspecs
        out_shape += [jax.ShapeDtypeStruct(w.shape, BF16) for w in (wg, wu, wd)]
    outs = pl.pallas_call(
        functools.partial(_ffn_kernel, final_norm=final_norm, emit_w16=emit_w16),
        grid=grid,
        in_specs=[
            pl.BlockSpec((tm, D_MODEL), lambda i, j: (i, 0)),
            pl.BlockSpec((1, D_MODEL), lambda i, j: (0, 0)),
            *wspecs,
            pl.BlockSpec((1, D_MODEL), lambda i, j: (0, 0)),
        ],
        out_specs=out_specs,
        out_shape=out_shape,
        scratch_shapes=[pltpu.VMEM((tm, D_MODEL), BF16)],
        compiler_params=_params("parallel", "arbitrary"),
        name="ffn_final" if final_norm else "ffn",
    )(x, g, wg, wu, wd, fg)
    return (outs[0], outs[1:]) if emit_w16 else outs[0]


def _ffn_ring_kernel(x_ref, g_ref, wg_hbm, wu_hbm, wd_hbm, fg_ref, o_ref, h_ref, wg_buf, wu_buf, wd_buf, sem,
                     *, tf, final_norm):
    nf = D_FF // tf

    def tile_copies(j, slot):
        off = pl.multiple_of(j * tf, tf)
        return (pltpu.make_async_copy(wg_hbm.at[:, pl.ds(off, tf)], wg_buf.at[slot], sem.at[slot, 0]),
                pltpu.make_async_copy(wu_hbm.at[:, pl.ds(off, tf)], wu_buf.at[slot], sem.at[slot, 1]),
                pltpu.make_async_copy(wd_hbm.at[pl.ds(off, tf), :], wd_buf.at[slot], sem.at[slot, 2]))

    for c in tile_copies(0, 0):
        c.start()
    x = x_ref[...]
    h_ref[...] = _rms(x, g_ref[...]).astype(BF16)
    o_ref[...] = x

    def body(j, carry):
        slot = j % 2

        @pl.when(j + 1 < nf)
        def _():
            for c in tile_copies(j + 1, 1 - slot):
                c.start()

        for c in tile_copies(j, slot):
            c.wait()
        h = h_ref[...]
        gate = jnp.dot(h, wg_buf[slot], preferred_element_type=F32)
        up = jnp.dot(h, wu_buf[slot], preferred_element_type=F32)
        a = (gate * _sigmoid(gate)) * (up * 0.5)
        o_ref[...] += jnp.dot(a.astype(BF16), wd_buf[slot], preferred_element_type=F32)
        return carry

    lax.fori_loop(0, nf, body, 0)
    if final_norm:
        o_ref[...] = _rms(o_ref[...], fg_ref[...])


def _ffn_ring(x, g, wg, wu, wd, fg, *, tm, tf, final_norm):
    m = x.shape[0]
    assert wg.dtype == BF16 and D_FF % tf == 0
    hbm = pl.BlockSpec(memory_space=pl.ANY)
    return pl.pallas_call(
        functools.partial(_ffn_ring_kernel, tf=tf, final_norm=final_norm),
        grid=(m // tm,),
        in_specs=[
            pl.BlockSpec((tm, D_MODEL), lambda i: (i, 0)),
            pl.BlockSpec((1, D_MODEL), lambda i: (0, 0)),
            hbm, hbm, hbm,
            pl.BlockSpec((1, D_MODEL), lambda i: (0, 0)),
        ],
        out_specs=pl.BlockSpec((tm, D_MODEL), lambda i: (i, 0)),
        out_shape=jax.ShapeDtypeStruct((m, D_MODEL), F32),
        scratch_shapes=[
            pltpu.VMEM((tm, D_MODEL), BF16),
            pltpu.VMEM((2, D_MODEL, tf), BF16),
            pltpu.VMEM((2, D_MODEL, tf), BF16),
            pltpu.VMEM((2, tf, D_MODEL), BF16),
            pltpu.SemaphoreType.DMA((2, 3)),
        ],
        compiler_params=_params("arbitrary"),
        name="ffn_ring_final" if final_norm else "ffn_ring",
    )(x, g, wg, wu, wd, fg)


def _qk_norm(x, g):
    outs = []
    for hh in range(x.shape[1] // LANES):
        xh = x[:, hh * LANES:(hh + 1) * LANES]
        lo = lax.broadcasted_iota(jnp.int32, xh.shape, 1) < QK_DIM
        x2 = xh * xh
        s_lo = jnp.sum(jnp.where(lo, x2, 0.0), axis=-1, keepdims=True)
        s_hi = jnp.sum(jnp.where(lo, 0.0, x2), axis=-1, keepdims=True)
        ms = jnp.where(lo, s_lo, s_hi) * (1.0 / QK_DIM)
        outs.append(xh * lax.rsqrt(ms + EPS) * g)
    return outs[0] if len(outs) == 1 else jnp.concatenate(outs, axis=1)


def _inproj_kernel(x_ref, g_ref, wq_ref, wk_ref, wv_ref, wa_ref, wc_ref, qg_ref, kg_ref,
                   q_ref, k_ref, vt_ref, u_ref, *rest, emit_w16):
    h_ref = rest[-1]
    j = pl.program_id(1)

    @pl.when(j == 0)
    def _():
        h_ref[...] = _rms(x_ref[...], g_ref[...]).astype(BF16)

    ws = [w_ref[...].astype(BF16) for w_ref in (wq_ref, wk_ref, wv_ref, wa_ref, wc_ref)]
    if emit_w16:
        for w16_ref, w in zip(rest[:5], ws):
            w16_ref[...] = w
    h = h_ref[...]
    q, k, v, ca, cg = (jnp.dot(h, w, preferred_element_type=F32) for w in ws)
    q_ref[...] = _qk_norm(q, qg_ref[...]).astype(BF16)
    k_ref[...] = _qk_norm(k, kg_ref[...]).astype(BF16)
    vt_ref[...] = v.T.astype(BF16)
    u_ref[...] = ca * _sigmoid(cg)


def _in_proj(x, g, ws, qg, kg, *, tm, tn, emit_w16=False):
    m = x.shape[0]
    nb = ATTN_WIDTH // tn
    assert not emit_w16 or m == tm
    if isinstance(ws, (tuple, list)):
        wspecs = [pl.BlockSpec((D_MODEL, tn), lambda i, j: (0, j))] * 5
    else:
        wspecs = [pl.BlockSpec((D_MODEL, tn), lambda i, j, c=c: (0, c * nb + j)) for c in range(5)]
        ws = [ws] * 5
    out_specs = [
        pl.BlockSpec((tm, tn), lambda i, j: (i, j)),
        pl.BlockSpec((tm, tn), lambda i, j: (i, j)),
        pl.BlockSpec((tn, tm), lambda i, j: (j, i)),
        pl.BlockSpec((tm, tn), lambda i, j: (i, j)),
    ]
    out_shape = [
        jax.ShapeDtypeStruct((m, ATTN_WIDTH), BF16),
        jax.ShapeDtypeStruct((m, ATTN_WIDTH), BF16),
        jax.ShapeDtypeStruct((ATTN_WIDTH, m), BF16),
        jax.ShapeDtypeStruct((m, CONV_WIDTH), F32),
    ]
    if emit_w16:
        out_specs += [pl.BlockSpec((D_MODEL, tn), lambda i, j: (0, j))] * 5
        out_shape += [jax.ShapeDtypeStruct((D_MODEL, ATTN_WIDTH), BF16)] * 5
    outs = pl.pallas_call(
        functools.partial(_inproj_kernel, emit_w16=emit_w16),
        grid=(m // tm, nb),
        in_specs=[
            pl.BlockSpec((tm, D_MODEL), lambda i, j: (i, 0)),
            pl.BlockSpec((1, D_MODEL), lambda i, j: (0, 0)),
            *wspecs,
            pl.BlockSpec((1, LANES), lambda i, j: (0, 0)),
            pl.BlockSpec((1, LANES), lambda i, j: (0, 0)),
        ],
        out_specs=out_specs,
        out_shape=out_shape,
        scratch_shapes=[pltpu.VMEM((tm, D_MODEL), BF16)],
        compiler_params=_params("parallel", "arbitrary"),
        name="in_proj",
    )(x, g, *ws, qg, kg)
    return (outs[:4], outs[4:]) if emit_w16 else outs


def _attn_kernel(q_ref, k_ref, vt_ref, km_ref, vmt_ref, lq1_ref, lk1_ref, lq2_ref, lk2_ref, sg_ref,
                 *rest, tq, tk, cw, n_cast, tiles_per_step):
    w32_refs, o_ref, w16_refs = rest[:n_cast], rest[n_cast], rest[n_cast + 1:2 * n_cast + 1]
    m_sc, acc_sc, s0_sc, s1_sc = rest[2 * n_cast + 1:]
    n_chunks = 2 * tq // cw
    i = pl.program_id(2)

    def scores(blk, s_ref, qs):
        kb = k_ref[blk * tk:(blk + 1) * tk, :]
        for c in range(n_chunks):
            s_ref[c] = jnp.dot(kb, qs[:, c * cw:(c + 1) * cw], preferred_element_type=F32)

    def with_ones(vt):
        return jnp.concatenate([vt, jnp.ones((DENOM_ROWS, vt.shape[1]), BF16)], axis=0)

    def online_update(s, vt, m_old, acc_old):
        m_new = jnp.maximum(m_old, jnp.max(s, axis=0, keepdims=True))
        alpha = jnp.exp2(m_old - m_new)
        p = jnp.exp2(s - m_new)
        return m_new, alpha * acc_old + jnp.dot(with_ones(vt), p.astype(BF16), preferred_element_type=F32)

    def softmax_pv(blk, s_ref, diag):
        off = blk * tk
        for c in range(n_chunks):
            cs = slice(c * cw, (c + 1) * cw)
            nk, masked = tk, False
            if diag is not None:
                q0 = (c * cw) % tq
                nk = min(tk, q0 + cw - diag * tk)
                if nk <= 0:
                    continue
                masked = (diag * tk + nk - 1) // CHUNK > q0 // CHUNK
            s = s_ref[c, 0:nk, :]
            if masked:
                kchunk = (lax.broadcasted_iota(jnp.int32, (nk, cw), 0) + diag * tk) // CHUNK
                qchunk = (lax.broadcasted_iota(jnp.int32, (nk, cw), 1) + q0) // CHUNK
                s = jnp.where(kchunk <= qchunk, s, NEG_INF)
            m_sc[:, cs], acc_sc[:, cs] = online_update(s, vt_ref[:, off:off + nk], m_sc[:, cs], acc_sc[:, cs])

    def run(tile, r0):
        qt = q_ref[r0:r0 + tq, :].astype(F32).T
        row = lax.broadcasted_iota(jnp.int32, qt.shape, 0)
        qs = jnp.concatenate([jnp.where(row < QK_DIM, qt, 0.0),
                              jnp.where(row < QK_DIM, 0.0, qt)], axis=1).astype(BF16)
        per_tile = tq // tk
        blocks = [(b, None) for b in range(tile * per_tile)] + [(tile * per_tile + d, d) for d in range(per_tile)]
        bufs = (s0_sc, s1_sc)
        scores(blocks[0][0], bufs[0], qs)
        s = jnp.dot(km_ref[...], qs, preferred_element_type=F32)
        m_sc[...], acc_sc[...] = online_update(s, vmt_ref[...], jnp.full(m_sc.shape, NEG_INF, F32),
                                               jnp.zeros(acc_sc.shape, F32))
        for t, (blk, diag) in enumerate(blocks):
            if t + 1 < len(blocks):
                scores(blocks[t + 1][0], bufs[(t + 1) % 2], qs)
            softmax_pv(blk, bufs[t % 2], diag)

        lam = (jnp.exp(jnp.sum(lq1_ref[...] * lk1_ref[...], axis=-1, keepdims=True))
               - jnp.exp(jnp.sum(lq2_ref[...] * lk2_ref[...], axis=-1, keepdims=True)) + LAM_INIT)
        acc = acc_sc[...]
        o = acc[:HEAD_DIM] * (1.0 / acc[HEAD_DIM:HEAD_DIM + 1])
        od = o[:, :tq] - lam * o[:, tq:]
        ms = jnp.mean(od * od, axis=0, keepdims=True)
        on = od * lax.rsqrt(ms + EPS) * sg_ref[...] * (1.0 - LAM_INIT)
        o_ref[r0:r0 + tq, :] = on.T.astype(BF16)

    def step(first_tile):
        for w32, w16 in zip(w32_refs, w16_refs):
            w16[...] = w32[...].astype(BF16)
        for t in range(tiles_per_step):
            run(first_tile + t, t * tq)

    for n in range(SEQ // (tq * tiles_per_step)):
        pl.when(i == n)(functools.partial(step, n * tiles_per_step))


def _attention(q, k, vt, km, vmt, lq1, lk1, lq2, lk2, sg, cast_ws, *, batch, tq, tk, cw, tiles_per_step):
    assert tq % tk == 0 and tk % cw == 0 and cw % CHUNK == 0
    nq = SEQ // (tq * tiles_per_step)
    steps = batch * N_HEADS * nq
    lspec = pl.BlockSpec((1, QK_DIM), lambda b, h, i: (0, 0))

    def slab(w):
        rep = next(r for r in (1, 2, 4, 8) if w.shape[0] * r % (BF16_ROWS * steps) == 0)
        return pl.BlockSpec((w.shape[0] * rep // steps, w.shape[1]),
                            lambda b, h, i: (((b * N_HEADS + h) * nq + i) // rep, 0))

    outs = pl.pallas_call(
        functools.partial(_attn_kernel, tq=tq, tk=tk, cw=cw, n_cast=len(cast_ws), tiles_per_step=tiles_per_step),
        grid=(batch, N_HEADS, nq),
        in_specs=[
            pl.BlockSpec((tq * tiles_per_step, HEAD_DIM), lambda b, h, i: (b * nq + i, h)),
            pl.BlockSpec((SEQ, HEAD_DIM), lambda b, h, i: (b, h)),
            pl.BlockSpec((HEAD_DIM, SEQ), lambda b, h, i: (h, b)),
            pl.BlockSpec((N_META, HEAD_DIM), lambda b, h, i: (0, h)),
            pl.BlockSpec((HEAD_DIM, N_META), lambda b, h, i: (h, 0)),
            lspec, lspec, lspec, lspec,
            pl.BlockSpec((HEAD_DIM, 1), lambda b, h, i: (0, 0)),
        ] + [slab(w) for w in cast_ws],
        out_specs=[pl.BlockSpec((tq * tiles_per_step, HEAD_DIM), lambda b, h, i: (b * nq + i, h))]
        + [slab(w) for w in cast_ws],
        out_shape=[jax.ShapeDtypeStruct((batch * SEQ, ATTN_WIDTH), BF16)]
        + [jax.ShapeDtypeStruct(w.shape, BF16) for w in cast_ws],
        scratch_shapes=[
            pltpu.VMEM((1, 2 * tq), F32),
            pltpu.VMEM((HEAD_DIM + DENOM_ROWS, 2 * tq), F32),
            pltpu.VMEM((2 * tq // cw, tk, cw), F32),
            pltpu.VMEM((2 * tq // cw, tk, cw), F32),
        ],
        compiler_params=_params("parallel", "parallel", "arbitrary"),
        name="diff_attn",
    )(q, k, vt, km, vmt, lq1, lk1, lq2, lk2, sg, *cast_ws)
    return outs[0], outs[1:]


def _conv_kernel(u_ref, prev_ref, halo0_ref, w_ref, b_ref, y_ref, win_sc, shift_sc, *, tl, rc, rb):
    i = pl.program_id(1)

    @pl.when(i == 0)
    def _():
        win_sc[0:HALO, :] = halo0_ref[...]

    @pl.when(i > 0)
    def _():
        win_sc[0:HALO, :] = prev_ref[...]

    win_sc[HALO:HALO + tl, :] = u_ref[...]

    nsh = HALO + tl - SUBLANES
    for s in range(1, SUBLANES):
        shift_sc[s - 1] = win_sc[s:s + nsh, :]

    first = HALO - (CONV_KERNEL - 1)
    @pl.loop(0, CONV_WIDTH // LANES)
    def _(c):
        cs = pl.ds(pl.multiple_of(c * LANES, LANES), LANES)
        for r0 in range(0, tl // rc, rb):
            accs = [jnp.broadcast_to(b_ref[:, cs], (rc, LANES))] * rb
            for t in range(CONV_KERNEL):
                wt = w_ref[t:t + 1, cs]
                for k in range(rb):
                    lo = (r0 + k) * rc + first + t
                    s, al = lo % SUBLANES, lo - lo % SUBLANES
                    tap = win_sc[al:al + rc, cs] if s == 0 else shift_sc[s - 1, al:al + rc, cs]
                    accs[k] = accs[k] + wt * tap
            for k in range(rb):
                y_ref[(r0 + k) * rc:(r0 + k + 1) * rc, cs] = accs[k]


def _conv(u, halo0, w, b, *, batch, tl, rc, rb):
    nt = SEQ // tl
    hb = tl // HALO
    return pl.pallas_call(
        functools.partial(_conv_kernel, tl=tl, rc=rc, rb=rb),
        grid=(batch, nt),
        in_specs=[
            pl.BlockSpec((tl, CONV_WIDTH), lambda bb, i: (bb * nt + i, 0)),
            pl.BlockSpec((HALO, CONV_WIDTH), lambda bb, i: (jnp.maximum((bb * nt + i) * hb - 1, 0), 0)),
            pl.BlockSpec((HALO, CONV_WIDTH), lambda bb, i: (0, 0)),
            pl.BlockSpec((CONV_KERNEL, CONV_WIDTH), lambda bb, i: (0, 0)),
            pl.BlockSpec((1, CONV_WIDTH), lambda bb, i: (0, 0)),
        ],
        out_specs=pl.BlockSpec((tl, CONV_WIDTH), lambda bb, i: (bb * nt + i, 0)),
        out_shape=jax.ShapeDtypeStruct((batch * SEQ, CONV_WIDTH), F32),
        scratch_shapes=[
            pltpu.VMEM((HALO + tl, CONV_WIDTH), F32),
            pltpu.VMEM((SUBLANES - 1, HALO + tl - SUBLANES, CONV_WIDTH), F32),
        ],
        compiler_params=_params("parallel", "arbitrary"),
        name="conv",
    )(u, u, halo0, w, b)


def _outproj_kernel(x_ref, a_ref, y_ref, cng_ref, wa_ref, wc_ref, o_ref):
    att = jnp.dot(a_ref[...], wa_ref[...], preferred_element_type=F32)
    yn = _rms(y_ref[...], cng_ref[...])
    oc = (yn * _sigmoid(yn)).astype(BF16)
    o_ref[...] = x_ref[...] + att + jnp.dot(oc, wc_ref[...], preferred_element_type=F32)


def _out_proj(x, oa, y, cng, w_out, *, tm):
    m = x.shape[0]
    return pl.pallas_call(
        _outproj_kernel,
        grid=(m // tm,),
        in_specs=[
            pl.BlockSpec((tm, D_MODEL), lambda i: (i, 0)),
            pl.BlockSpec((tm, ATTN_WIDTH), lambda i: (i, 0)),
            pl.BlockSpec((tm, CONV_WIDTH), lambda i: (i, 0)),
            pl.BlockSpec((1, CONV_WIDTH), lambda i: (0, 0)),
            pl.BlockSpec((ATTN_WIDTH, D_MODEL), lambda i: (0, 0)),
            pl.BlockSpec((CONV_WIDTH, D_MODEL), lambda i: (1, 0)),
        ],
        out_specs=pl.BlockSpec((tm, D_MODEL), lambda i: (i, 0)),
        out_shape=jax.ShapeDtypeStruct((m, D_MODEL), F32),
        compiler_params=_params("parallel"),
        name="out_proj",
    )(x, oa, y, cng, w_out, w_out)


def kernel(x, meta_tokens, ffn1_norm_g, ffn1_w_gate, ffn1_w_up, ffn1_w_down, mix_norm_g, w_in, q_norm_g,
           k_norm_g, lambda_q1, lambda_k1, lambda_q2, lambda_k2, attn_subln_g, conv_w, conv_b, conv_norm_g,
           w_out, ffn2_norm_g, ffn2_w_gate, ffn2_w_up, ffn2_w_down, final_norm_g):
    batch = x.shape[0]
    xr = x.reshape(batch * SEQ, D_MODEL)

    w1g, w1u, w1d = ffn1_w_gate[0], ffn1_w_up[0], ffn1_w_down[0]
    w2g, w2u, w2d = ffn2_w_gate[0], ffn2_w_up[0], ffn2_w_down[0]
    win, wout = w_in[0], w_out[0]
    qg = jnp.tile(q_norm_g[0] * (QK_DIM ** -0.5 * LOG2E), 2)[None, :]
    kg = jnp.tile(k_norm_g[0], 2)[None, :]
    sg = attn_subln_g[0][:, None]
    cw = conv_w[0].reshape(CONV_KERNEL, CONV_WIDTH)

    m1, (w1g, w1u, w1d) = _ffn(meta_tokens, ffn1_norm_g, w1g, w1u, w1d, final_norm_g,
                               tm=N_META, tf=META_FFN_TF, final_norm=False, emit_w16=True)
    (_, km, vmt, um), win16 = _in_proj(m1, mix_norm_g, win, qg, kg, tm=N_META, tn=PROJ_TN, emit_w16=True)
    halo0 = jnp.concatenate([jnp.zeros((HALO - N_META, CONV_WIDTH), F32), um], axis=0)

    x1 = _ffn_ring(xr, ffn1_norm_g, w1g, w1u, w1d, final_norm_g, tm=FFN_TM, tf=FFN_TF, final_norm=False)
    q, k, vt, u = _in_proj(x1, mix_norm_g, win16, qg, kg, tm=PROJ_TM, tn=PROJ_TN)
    oa, (w2g, w2u, w2d, wout) = _attention(q, k, vt, km, vmt, lambda_q1, lambda_k1, lambda_q2, lambda_k2, sg,
                                           [w2g, w2u, w2d, wout], batch=batch,
                                           tq=ATTN_TQ, tk=ATTN_TK, cw=ATTN_CW,
                                           tiles_per_step=ATTN_TILES_PER_STEP)
    yc = _conv(u, halo0, cw, conv_b, batch=batch, tl=CONV_TL, rc=CONV_RC, rb=CONV_RB)
    x2 = _out_proj(x1, oa, yc, conv_norm_g, wout, tm=OUT_TM)
    y = _ffn_ring(x2, ffn2_norm_g, w2g, w2u, w2d, final_norm_g, tm=FFN_TM, tf=FFN_TF, final_norm=True)
    return y.reshape(batch, SEQ, D_MODEL)
```

```python
import functools
import math

import jax
import jax.numpy as jnp
from jax import lax
from jax.experimental import pallas as pl
from jax.experimental.pallas import tpu as pltpu

F32 = jnp.float32
BF16 = jnp.bfloat16

D_MODEL = 2048
SEQ = 4096
N_META = 16
CHUNK = 64
ATTN_WIDTH = 1024
CONV_WIDTH = 1024
HEAD_DIM = 128
N_HEADS = ATTN_WIDTH // HEAD_DIM
QK_DIM = HEAD_DIM // 2
CONV_KERNEL = 31
D_FF = 5632
EPS = 1e-6
NEG_INF = -1e30
LAM_INIT = 0.8 - 0.6 * math.exp(-0.3 * 1)

LOG2E = math.log2(math.e)

LANES = 128
SUBLANES = 8
BF16_ROWS = 16
DENOM_ROWS = 16
HALO = 32
VMEM_LIMIT = 56 * 1024 * 1024

FFN_TM, FFN_TF = 1024, 512
META_FFN_TF = 512
PROJ_TM, PROJ_TN = 1024, 256
ATTN_TQ, ATTN_TK, ATTN_CW = 1024, 512, 256
ATTN_TILES_PER_STEP = 2
CONV_TL, CONV_RC, CONV_RB = 512, 32, 8
OUT_TM = 512


def _params(*sem):
    return pltpu.CompilerParams(dimension_semantics=sem, vmem_limit_bytes=VMEM_LIMIT)


def _rms(x, g):
    ms = jnp.mean(x * x, axis=-1, keepdims=True)
    return x * lax.rsqrt(ms + EPS) * g


def _sigmoid(x):
    return 1.0 / (1.0 + jnp.exp(-x))


def _ffn_kernel(x_ref, g_ref, wg_ref, wu_ref, wd_ref, fg_ref, o_ref, *rest, final_norm, emit_w16):
    h_ref = rest[-1]
    j = pl.program_id(1)

    def hidden_tile(first):
        if first:
            h_ref[...] = _rms(x_ref[...], g_ref[...]).astype(BF16)
        wg, wu, wd = (w_ref[...].astype(BF16) for w_ref in (wg_ref, wu_ref, wd_ref))
        if emit_w16:
            for w16_ref, w in zip(rest[:3], (wg, wu, wd)):
                w16_ref[...] = w
        h = h_ref[...]
        gate = jnp.dot(h, wg, preferred_element_type=F32)
        up = jnp.dot(h, wu, preferred_element_type=F32)
        a = (gate * _sigmoid(gate)) * (up * 0.5)
        down = jnp.dot(a.astype(BF16), wd, preferred_element_type=F32)
        o_ref[...] = (x_ref[...] if first else o_ref[...]) + down

    pl.when(j == 0)(functools.partial(hidden_tile, True))
    pl.when(j > 0)(functools.partial(hidden_tile, False))

    if final_norm:
        @pl.when(j == pl.num_programs(1) - 1)
        def _():
            o_ref[...] = _rms(o_ref[...], fg_ref[...])


def _ffn(x, g, wg, wu, wd, fg, *, tm, tf, final_norm, emit_w16=False):
    m = x.shape[0]
    grid = (m // tm, D_FF // tf)
    assert not emit_w16 or grid[0] == 1
    wspecs = [
        pl.BlockSpec((D_MODEL, tf), lambda i, j: (0, j)),
        pl.BlockSpec((D_MODEL, tf), lambda i, j: (0, j)),
        pl.BlockSpec((tf, D_MODEL), lambda i, j: (j, 0)),
    ]
    out_specs = [pl.BlockSpec((tm, D_MODEL), lambda i, j: (i, 0))]
    out_shape = [jax.ShapeDtypeStruct((m, D_MODEL), F32)]
    if emit_w16:
        out_specs += wspecs
        out_shape += [jax.ShapeDtypeStruct(w.shape, BF16) for w in (wg, wu, wd)]
    outs = pl.pallas_call(
        functools.partial(_ffn_kernel, final_norm=final_norm, emit_w16=emit_w16),
        grid=grid,
        in_specs=[
            pl.BlockSpec((tm, D_MODEL), lambda i, j: (i, 0)),
            pl.BlockSpec((1, D_MODEL), lambda i, j: (0, 0)),
            *wspecs,
            pl.BlockSpec((1, D_MODEL), lambda i, j: (0, 0)),
        ],
        out_specs=out_specs,
        out_shape=out_shape,
        scratch_shapes=[pltpu.VMEM((tm, D_MODEL), BF16)],
        compiler_params=_params("parallel", "arbitrary"),
        name="ffn_final" if final_norm else "ffn",
    )(x, g, wg, wu, wd, fg)
    return (outs[0], outs[1:]) if emit_w16 else outs[0]


def _qk_norm(x, g):
    outs = []
    for hh in range(x.shape[1] // LANES):
        xh = x[:, hh * LANES:(hh + 1) * LANES]
        lo = lax.broadcasted_iota(jnp.int32, xh.shape, 1) < QK_DIM
        x2 = xh * xh
        s_lo = jnp.sum(jnp.where(lo, x2, 0.0), axis=-1, keepdims=True)
        s_hi = jnp.sum(jnp.where(lo, 0.0, x2), axis=-1, keepdims=True)
        ms = jnp.where(lo, s_lo, s_hi) * (1.0 / QK_DIM)
        outs.append(xh * lax.rsqrt(ms + EPS) * g)
    return outs[0] if len(outs) == 1 else jnp.concatenate(outs, axis=1)


def _inproj_kernel(x_ref, g_ref, wq_ref, wk_ref, wv_ref, wa_ref, wc_ref, qg_ref, kg_ref,
                   q_ref, k_ref, vt_ref, u_ref, *rest, emit_w16):
    h_ref = rest[-1]
    j = pl.program_id(1)

    def column_block(first):
        if first:
            h_ref[...] = _rms(x_ref[...], g_ref[...]).astype(BF16)
        ws = [w_ref[...].astype(BF16) for w_ref in (wq_ref, wk_ref, wv_ref, wa_ref, wc_ref)]
        if emit_w16:
            for w16_ref, w in zip(rest[:5], ws):
                w16_ref[...] = w
        h = h_ref[...]
        q, k, v, ca, cg = (jnp.dot(h, w, preferred_element_type=F32) for w in ws)
        q_ref[...] = _qk_norm(q, qg_ref[...]).astype(BF16)
        k_ref[...] = _qk_norm(k, kg_ref[...]).astype(BF16)
        vt_ref[...] = v.T.astype(BF16)
        u_ref[...] = ca * _sigmoid(cg)

    pl.when(j == 0)(functools.partial(column_block, True))
    pl.when(j > 0)(functools.partial(column_block, False))


def _in_proj(x, g, ws, qg, kg, *, tm, tn, emit_w16=False):
    m = x.shape[0]
    nb = ATTN_WIDTH // tn
    assert not emit_w16 or m == tm
    if isinstance(ws, (tuple, list)):
        wspecs = [pl.BlockSpec((D_MODEL, tn), lambda i, j: (0, j))] * 5
    else:
        wspecs = [pl.BlockSpec((D_MODEL, tn), lambda i, j, c=c: (0, c * nb + j)) for c in range(5)]
        ws = [ws] * 5
    out_specs = [
        pl.BlockSpec((tm, tn), lambda i, j: (i, j)),
        pl.BlockSpec((tm, tn), lambda i, j: (i, j)),
        pl.BlockSpec((tn, tm), lambda i, j: (j, i)),
        pl.BlockSpec((tm, tn), lambda i, j: (i, j)),
    ]
    out_shape = [
        jax.ShapeDtypeStruct((m, ATTN_WIDTH), BF16),
        jax.ShapeDtypeStruct((m, ATTN_WIDTH), BF16),
        jax.ShapeDtypeStruct((ATTN_WIDTH, m), BF16),
        jax.ShapeDtypeStruct((m, CONV_WIDTH), F32),
    ]
    if emit_w16:
        out_specs += [pl.BlockSpec((D_MODEL, tn), lambda i, j: (0, j))] * 5
        out_shape += [jax.ShapeDtypeStruct((D_MODEL, ATTN_WIDTH), BF16)] * 5
    outs = pl.pallas_call(
        functools.partial(_inproj_kernel, emit_w16=emit_w16),
        grid=(m // tm, nb),
        in_specs=[
            pl.BlockSpec((tm, D_MODEL), lambda i, j: (i, 0)),
            pl.BlockSpec((1, D_MODEL), lambda i, j: (0, 0)),
            *wspecs,
            pl.BlockSpec((1, LANES), lambda i, j: (0, 0)),
            pl.BlockSpec((1, LANES), lambda i, j: (0, 0)),
        ],
        out_specs=out_specs,
        out_shape=out_shape,
        scratch_shapes=[pltpu.VMEM((tm, D_MODEL), BF16)],
        compiler_params=_params("parallel", "arbitrary"),
        name="in_proj",
    )(x, g, *ws, qg, kg)
    return (outs[:4], outs[4:]) if emit_w16 else outs


def _attn_kernel(q_ref, k_ref, vt_ref, km_ref, vmt_ref, lq1_ref, lk1_ref, lq2_ref, lk2_ref, sg_ref,
                 *rest, tq, tk, cw, n_cast, tiles_per_step):
    w32_refs, o_ref, w16_refs = rest[:n_cast], rest[n_cast], rest[n_cast + 1:2 * n_cast + 1]
    m_sc, acc_sc, s0_sc, s1_sc = rest[2 * n_cast + 1:]
    n_chunks = 2 * tq // cw
    i = pl.program_id(2)

    def scores(blk, s_ref, qs):
        kb = k_ref[blk * tk:(blk + 1) * tk, :]
        for c in range(n_chunks):
            s_ref[c] = jnp.dot(kb, qs[:, c * cw:(c + 1) * cw], preferred_element_type=F32)

    def with_ones(vt):
        return jnp.concatenate([vt, jnp.ones((DENOM_ROWS, vt.shape[1]), BF16)], axis=0)

    def online_update(s, vt, m_old, acc_old):
        m_new = jnp.maximum(m_old, jnp.max(s, axis=0, keepdims=True))
        alpha = jnp.exp2(m_old - m_new)
        p = jnp.exp2(s - m_new)
        return m_new, alpha * acc_old + jnp.dot(with_ones(vt), p.astype(BF16), preferred_element_type=F32)

    def softmax_pv(blk, s_ref, diag):
        off = blk * tk
        for c in range(n_chunks):
            cs = slice(c * cw, (c + 1) * cw)
            nk, masked = tk, False
            if diag is not None:
                q0 = (c * cw) % tq
                nk = min(tk, q0 + cw - diag * tk)
                if nk <= 0:
                    continue
                masked = (diag * tk + nk - 1) // CHUNK > q0 // CHUNK
            s = s_ref[c, 0:nk, :]
            if masked:
                kchunk = (lax.broadcasted_iota(jnp.int32, (nk, cw), 0) + diag * tk) // CHUNK
                qchunk = (lax.broadcasted_iota(jnp.int32, (nk, cw), 1) + q0) // CHUNK
                s = jnp.where(kchunk <= qchunk, s, NEG_INF)
            m_sc[:, cs], acc_sc[:, cs] = online_update(s, vt_ref[:, off:off + nk], m_sc[:, cs], acc_sc[:, cs])

    def run(tile, r0):
        qt = q_ref[r0:r0 + tq, :].astype(F32).T
        row = lax.broadcasted_iota(jnp.int32, qt.shape, 0)
        qs = jnp.concatenate([jnp.where(row < QK_DIM, qt, 0.0),
                              jnp.where(row < QK_DIM, 0.0, qt)], axis=1).astype(BF16)
        per_tile = tq // tk
        blocks = [(b, None) for b in range(tile * per_tile)] + [(tile * per_tile + d, d) for d in range(per_tile)]
        bufs = (s0_sc, s1_sc)
        scores(blocks[0][0], bufs[0], qs)
        s = jnp.dot(km_ref[...], qs, preferred_element_type=F32)
        m_sc[...], acc_sc[...] = online_update(s, vmt_ref[...], jnp.full(m_sc.shape, NEG_INF, F32),
                                               jnp.zeros(acc_sc.shape, F32))
        for t, (blk, diag) in enumerate(blocks):
            if t + 1 < len(blocks):
                scores(blocks[t + 1][0], bufs[(t + 1) % 2], qs)
            softmax_pv(blk, bufs[t % 2], diag)

        lam = (jnp.exp(jnp.sum(lq1_ref[...] * lk1_ref[...], axis=-1, keepdims=True))
               - jnp.exp(jnp.sum(lq2_ref[...] * lk2_ref[...], axis=-1, keepdims=True)) + LAM_INIT)
        acc = acc_sc[...]
        o = acc[:HEAD_DIM] * (1.0 / acc[HEAD_DIM:HEAD_DIM + 1])
        od = o[:, :tq] - lam * o[:, tq:]
        ms = jnp.mean(od * od, axis=0, keepdims=True)
        on = od * lax.rsqrt(ms + EPS) * sg_ref[...] * (1.0 - LAM_INIT)
        o_ref[r0:r0 + tq, :] = on.T.astype(BF16)

    def step(first_tile):
        for w32, w16 in zip(w32_refs, w16_refs):
            w16[...] = w32[...].astype(BF16)
        for t in range(tiles_per_step):
            run(first_tile + t, t * tq)

    for n in range(SEQ // (tq * tiles_per_step)):
        pl.when(i == n)(functools.partial(step, n * tiles_per_step))


def _attention(q, k, vt, km, vmt, lq1, lk1, lq2, lk2, sg, cast_ws, *, batch, tq, tk, cw, tiles_per_step):
    assert tq % tk == 0 and tk % cw == 0 and cw % CHUNK == 0
    nq = SEQ // (tq * tiles_per_step)
    steps = batch * N_HEADS * nq
    lspec = pl.BlockSpec((1, QK_DIM), lambda b, h, i: (0, 0))

    def slab(w):
        rep = next(r for r in (1, 2, 4, 8) if w.shape[0] * r % (BF16_ROWS * steps) == 0)
        return pl.BlockSpec((w.shape[0] * rep // steps, w.shape[1]),
                            lambda b, h, i: (((b * N_HEADS + h) * nq + i) // rep, 0))

    outs = pl.pallas_call(
        functools.partial(_attn_kernel, tq=tq, tk=tk, cw=cw, n_cast=len(cast_ws), tiles_per_step=tiles_per_step),
        grid=(batch, N_HEADS, nq),
        in_specs=[
            pl.BlockSpec((tq * tiles_per_step, HEAD_DIM), lambda b, h, i: (b * nq + i, h)),
            pl.BlockSpec((SEQ, HEAD_DIM), lambda b, h, i: (b, h)),
            pl.BlockSpec((HEAD_DIM, SEQ), lambda b, h, i: (h, b)),
            pl.BlockSpec((N_META, HEAD_DIM), lambda b, h, i: (0, h)),
            pl.BlockSpec((HEAD_DIM, N_META), lambda b, h, i: (h, 0)),
            lspec, lspec, lspec, lspec,
            pl.BlockSpec((HEAD_DIM, 1), lambda b, h, i: (0, 0)),
        ] + [slab(w) for w in cast_ws],
        out_specs=[pl.BlockSpec((tq * tiles_per_step, HEAD_DIM), lambda b, h, i: (b * nq + i, h))]
        + [slab(w) for w in cast_ws],
        out_shape=[jax.ShapeDtypeStruct((batch * SEQ, ATTN_WIDTH), BF16)]
        + [jax.ShapeDtypeStruct(w.shape, BF16) for w in cast_ws],
        scratch_shapes=[
            pltpu.VMEM((1, 2 * tq), F32),
            pltpu.VMEM((HEAD_DIM + DENOM_ROWS, 2 * tq), F32),
            pltpu.VMEM((2 * tq // cw, tk, cw), F32),
            pltpu.VMEM((2 * tq // cw, tk, cw), F32),
        ],
        compiler_params=_params("parallel", "parallel", "arbitrary"),
        name="diff_attn",
    )(q, k, vt, km, vmt, lq1, lk1, lq2, lk2, sg, *cast_ws)
    return outs[0], outs[1:]


def _conv_kernel(u_ref, prev_ref, halo0_ref, w_ref, b_ref, y_ref, win_sc, shift_sc, *, tl, rc, rb):
    i = pl.program_id(1)

    @pl.when(i == 0)
    def _():
        win_sc[0:HALO, :] = halo0_ref[...]

    @pl.when(i > 0)
    def _():
        win_sc[0:HALO, :] = prev_ref[...]

    win_sc[HALO:HALO + tl, :] = u_ref[...]

    nsh = HALO + tl - SUBLANES
    for s in range(1, SUBLANES):
        shift_sc[s - 1] = win_sc[s:s + nsh, :]

    first = HALO - (CONV_KERNEL - 1)
    @pl.loop(0, CONV_WIDTH // LANES)
    def _(c):
        cs = pl.ds(pl.multiple_of(c * LANES, LANES), LANES)
        for r0 in range(0, tl // rc, rb):
            accs = [jnp.broadcast_to(b_ref[:, cs], (rc, LANES))] * rb
            for t in range(CONV_KERNEL):
                wt = w_ref[t:t + 1, cs]
                for k in range(rb):
                    lo = (r0 + k) * rc + first + t
                    s, al = lo % SUBLANES, lo - lo % SUBLANES
                    tap = win_sc[al:al + rc, cs] if s == 0 else shift_sc[s - 1, al:al + rc, cs]
                    accs[k] = accs[k] + wt * tap
            for k in range(rb):
                y_ref[(r0 + k) * rc:(r0 + k + 1) * rc, cs] = accs[k]


def _conv(u, halo0, w, b, *, batch, tl, rc, rb):
    nt = SEQ // tl
    hb = tl // HALO
    return pl.pallas_call(
        functools.partial(_conv_kernel, tl=tl, rc=rc, rb=rb),
        grid=(batch, nt),
        in_specs=[
            pl.BlockSpec((tl, CONV_WIDTH), lambda bb, i: (bb * nt + i, 0)),
            pl.BlockSpec((HALO, CONV_WIDTH), lambda bb, i: (jnp.maximum((bb * nt + i) * hb - 1, 0), 0)),
            pl.BlockSpec((HALO, CONV_WIDTH), lambda bb, i: (0, 0)),
            pl.BlockSpec((CONV_KERNEL, CONV_WIDTH), lambda bb, i: (0, 0)),
            pl.BlockSpec((1, CONV_WIDTH), lambda bb, i: (0, 0)),
        ],
        out_specs=pl.BlockSpec((tl, CONV_WIDTH), lambda bb, i: (bb * nt + i, 0)),
        out_shape=jax.ShapeDtypeStruct((batch * SEQ, CONV_WIDTH), F32),
        scratch_shapes=[
            pltpu.VMEM((HALO + tl, CONV_WIDTH), F32),
            pltpu.VMEM((SUBLANES - 1, HALO + tl - SUBLANES, CONV_WIDTH), F32),
        ],
        compiler_params=_params("parallel", "arbitrary"),
        name="conv",
    )(u, u, halo0, w, b)


def _outproj_kernel(x_ref, a_ref, y_ref, cng_ref, wa_ref, wc_ref, o_ref):
    att = jnp.dot(a_ref[...], wa_ref[...], preferred_element_type=F32)
    yn = _rms(y_ref[...], cng_ref[...])
    oc = (yn * _sigmoid(yn)).astype(BF16)
    o_ref[...] = x_ref[...] + att + jnp.dot(oc, wc_ref[...], preferred_element_type=F32)


def _out_proj(x, oa, y, cng, w_out, *, tm):
    m = x.shape[0]
    return pl.pallas_call(
        _outproj_kernel,
        grid=(m // tm,),
        in_specs=[
            pl.BlockSpec((tm, D_MODEL), lambda i: (i, 0)),
            pl.BlockSpec((tm, ATTN_WIDTH), lambda i: (i, 0)),
            pl.BlockSpec((tm, CONV_WIDTH), lambda i: (i, 0)),
            pl.BlockSpec((1, CONV_WIDTH), lambda i: (0, 0)),
            pl.BlockSpec((ATTN_WIDTH, D_MODEL), lambda i: (0, 0)),
            pl.BlockSpec((CONV_WIDTH, D_MODEL), lambda i: (1, 0)),
        ],
        out_specs=pl.BlockSpec((tm, D_MODEL), lambda i: (i, 0)),
        out_shape=jax.ShapeDtypeStruct((m, D_MODEL), F32),
        compiler_params=_params("parallel"),
        name="out_proj",
    )(x, oa, y, cng, w_out, w_out)


def kernel(x, meta_tokens, ffn1_norm_g, ffn1_w_gate, ffn1_w_up, ffn1_w_down, mix_norm_g, w_in, q_norm_g,
           k_norm_g, lambda_q1, lambda_k1, lambda_q2, lambda_k2, attn_subln_g, conv_w, conv_b, conv_norm_g,
           w_out, ffn2_norm_g, ffn2_w_gate, ffn2_w_up, ffn2_w_down, final_norm_g):
    batch = x.shape[0]
    xr = x.reshape(batch * SEQ, D_MODEL)

    w1g, w1u, w1d = ffn1_w_gate[0], ffn1_w_up[0], ffn1_w_down[0]
    w2g, w2u, w2d = ffn2_w_gate[0], ffn2_w_up[0], ffn2_w_down[0]
    win, wout = w_in[0], w_out[0]
    qg = jnp.tile(q_norm_g[0] * (QK_DIM ** -0.5 * LOG2E), 2)[None, :]
    kg = jnp.tile(k_norm_g[0], 2)[None, :]
    sg = attn_subln_g[0][:, None]
    cw = conv_w[0].reshape(CONV_KERNEL, CONV_WIDTH)

    m1, (w1g, w1u, w1d) = _ffn(meta_tokens, ffn1_norm_g, w1g, w1u, w1d, final_norm_g,
                               tm=N_META, tf=META_FFN_TF, final_norm=False, emit_w16=True)
    (_, km, vmt, um), win16 = _in_proj(m1, mix_norm_g, win, qg, kg, tm=N_META, tn=PROJ_TN, emit_w16=True)
    halo0 = jnp.concatenate([jnp.zeros((HALO - N_META, CONV_WIDTH), F32), um], axis=0)

    x1 = _ffn(xr, ffn1_norm_g, w1g, w1u, w1d, final_norm_g, tm=FFN_TM, tf=FFN_TF, final_norm=False)
    q, k, vt, u = _in_proj(x1, mix_norm_g, win16, qg, kg, tm=PROJ_TM, tn=PROJ_TN)
    oa, (w2g, w2u, w2d, wout) = _attention(q, k, vt, km, vmt, lambda_q1, lambda_k1, lambda_q2, lambda_k2, sg,
                                           [w2g, w2u, w2d, wout], batch=batch,
                                           tq=ATTN_TQ, tk=ATTN_TK, cw=ATTN_CW,
                                           tiles_per_step=ATTN_TILES_PER_STEP)
    yc = _conv(u, halo0, cw, conv_b, batch=batch, tl=CONV_TL, rc=CONV_RC, rb=CONV_RB)
    x2 = _out_proj(x1, oa, yc, conv_norm_g, wout, tm=OUT_TM)
    y = _ffn(x2, ffn2_norm_g, w2g, w2u, w2d, final_norm_g, tm=FFN_TM, tf=FFN_TF, final_norm=True)
    return y.reshape(batch, SEQ, D_MODEL)
```

```python
import functools
import math

import jax
import jax.numpy as jnp
from jax import lax
from jax.experimental import pallas as pl
from jax.experimental.pallas import tpu as pltpu

F32 = jnp.float32
BF16 = jnp.bfloat16

D_MODEL = 2048
SEQ = 4096
N_META = 16
CHUNK = 64
ATTN_WIDTH = 1024
CONV_WIDTH = 1024
HEAD_DIM = 128
N_HEADS = ATTN_WIDTH // HEAD_DIM
QK_DIM = HEAD_DIM // 2
CONV_KERNEL = 31
D_FF = 5632
EPS = 1e-6
NEG_INF = -1e30
LAM_INIT = 0.8 - 0.6 * math.exp(-0.3 * 1)

LOG2E = math.log2(math.e)

LANES = 128
SUBLANES = 8
BF16_ROWS = 16
DENOM_ROWS = 16
HALO = 32
VMEM_LIMIT = 56 * 1024 * 1024

FFN_TM, FFN_TF = 1024, 512
META_FFN_TF = 512
PROJ_TM, PROJ_TN = 1024, 256
ATTN_TQ, ATTN_TK, ATTN_CW = 1024, 512, 256
ATTN_TILES_PER_STEP = 2
CONV_TL, CONV_RC, CONV_RB = 512, 32, 8
OUT_TM = 512


def _params(*sem):
    return pltpu.CompilerParams(dimension_semantics=sem, vmem_limit_bytes=VMEM_LIMIT)


def _rms(x, g):
    ms = jnp.mean(x * x, axis=-1, keepdims=True)
    return x * lax.rsqrt(ms + EPS) * g


def _sigmoid(x):
    return 1.0 / (1.0 + jnp.exp(-x))


def _ffn_kernel(x_ref, g_ref, wg_ref, wu_ref, wd_ref, fg_ref, o_ref, *rest, final_norm, emit_w16):
    h_ref = rest[-1]
    j = pl.program_id(1)

    def hidden_tile(first, last):
        if first:
            h_ref[...] = _rms(x_ref[...], g_ref[...]).astype(BF16)
        wg, wu, wd = (w_ref[...].astype(BF16) for w_ref in (wg_ref, wu_ref, wd_ref))
        if emit_w16:
            for w16_ref, w in zip(rest[:3], (wg, wu, wd)):
                w16_ref[...] = w
        h = h_ref[...]
        gate = jnp.dot(h, wg, preferred_element_type=F32)
        up = jnp.dot(h, wu, preferred_element_type=F32)
        a = (gate * _sigmoid(gate)) * (up * 0.5)
        down = jnp.dot(a.astype(BF16), wd, preferred_element_type=F32)
        acc = (x_ref[...] if first else o_ref[...]) + down
        o_ref[...] = _rms(acc, fg_ref[...]) if last and final_norm else acc

    last_j = pl.num_programs(1) - 1
    pl.when(j == 0)(functools.partial(hidden_tile, True, False))
    if final_norm:
        pl.when((j > 0) & (j < last_j))(functools.partial(hidden_tile, False, False))
        pl.when(j == last_j)(functools.partial(hidden_tile, False, True))
    else:
        pl.when(j > 0)(functools.partial(hidden_tile, False, False))


def _ffn(x, g, wg, wu, wd, fg, *, tm, tf, final_norm, emit_w16=False):
    m = x.shape[0]
    grid = (m // tm, D_FF // tf)
    assert not emit_w16 or grid[0] == 1
    wspecs = [
        pl.BlockSpec((D_MODEL, tf), lambda i, j: (0, j)),
        pl.BlockSpec((D_MODEL, tf), lambda i, j: (0, j)),
        pl.BlockSpec((tf, D_MODEL), lambda i, j: (j, 0)),
    ]
    out_specs = [pl.BlockSpec((tm, D_MODEL), lambda i, j: (i, 0))]
    out_shape = [jax.ShapeDtypeStruct((m, D_MODEL), F32)]
    if emit_w16:
        out_specs += wspecs
        out_shape += [jax.ShapeDtypeStruct(w.shape, BF16) for w in (wg, wu, wd)]
    outs = pl.pallas_call(
        functools.partial(_ffn_kernel, final_norm=final_norm, emit_w16=emit_w16),
        grid=grid,
        in_specs=[
            pl.BlockSpec((tm, D_MODEL), lambda i, j: (i, 0)),
            pl.BlockSpec((1, D_MODEL), lambda i, j: (0, 0)),
            *wspecs,
            pl.BlockSpec((1, D_MODEL), lambda i, j: (0, 0)),
        ],
        out_specs=out_specs,
        out_shape=out_shape,
        scratch_shapes=[pltpu.VMEM((tm, D_MODEL), BF16)],
        compiler_params=_params("parallel", "arbitrary"),
        name="ffn_final" if final_norm else "ffn",
    )(x, g, wg, wu, wd, fg)
    return (outs[0], outs[1:]) if emit_w16 else outs[0]


def _qk_norm(x, g):
    outs = []
    for hh in range(x.shape[1] // LANES):
        xh = x[:, hh * LANES:(hh + 1) * LANES]
        lo = lax.broadcasted_iota(jnp.int32, xh.shape, 1) < QK_DIM
        x2 = xh * xh
        s_lo = jnp.sum(jnp.where(lo, x2, 0.0), axis=-1, keepdims=True)
        s_hi = jnp.sum(jnp.where(lo, 0.0, x2), axis=-1, keepdims=True)
        ms = jnp.where(lo, s_lo, s_hi) * (1.0 / QK_DIM)
        outs.append(xh * lax.rsqrt(ms + EPS) * g)
    return outs[0] if len(outs) == 1 else jnp.concatenate(outs, axis=1)


def _inproj_kernel(x_ref, g_ref, wq_ref, wk_ref, wv_ref, wa_ref, wc_ref, qg_ref, kg_ref,
                   q_ref, k_ref, vt_ref, u_ref, *rest, emit_w16):
    h_ref = rest[-1]
    j = pl.program_id(1)

    def column_block(first):
        if first:
            h_ref[...] = _rms(x_ref[...], g_ref[...]).astype(BF16)
        ws = [w_ref[...].astype(BF16) for w_ref in (wq_ref, wk_ref, wv_ref, wa_ref, wc_ref)]
        if emit_w16:
            for w16_ref, w in zip(rest[:5], ws):
                w16_ref[...] = w
        h = h_ref[...]
        q, k, v, ca, cg = (jnp.dot(h, w, preferred_element_type=F32) for w in ws)
        q_ref[...] = _qk_norm(q, qg_ref[...]).astype(BF16)
        k_ref[...] = _qk_norm(k, kg_ref[...]).astype(BF16)
        vt_ref[...] = v.T.astype(BF16)
        u_ref[...] = ca * _sigmoid(cg)

    pl.when(j == 0)(functools.partial(column_block, True))
    pl.when(j > 0)(functools.partial(column_block, False))


def _in_proj(x, g, ws, qg, kg, *, tm, tn, emit_w16=False):
    m = x.shape[0]
    nb = ATTN_WIDTH // tn
    assert not emit_w16 or m == tm
    if isinstance(ws, (tuple, list)):
        wspecs = [pl.BlockSpec((D_MODEL, tn), lambda i, j: (0, j))] * 5
    else:
        wspecs = [pl.BlockSpec((D_MODEL, tn), lambda i, j, c=c: (0, c * nb + j)) for c in range(5)]
        ws = [ws] * 5
    out_specs = [
        pl.BlockSpec((tm, tn), lambda i, j: (i, j)),
        pl.BlockSpec((tm, tn), lambda i, j: (i, j)),
        pl.BlockSpec((tn, tm), lambda i, j: (j, i)),
        pl.BlockSpec((tm, tn), lambda i, j: (i, j)),
    ]
    out_shape = [
        jax.ShapeDtypeStruct((m, ATTN_WIDTH), BF16),
        jax.ShapeDtypeStruct((m, ATTN_WIDTH), BF16),
        jax.ShapeDtypeStruct((ATTN_WIDTH, m), BF16),
        jax.ShapeDtypeStruct((m, CONV_WIDTH), F32),
    ]
    if emit_w16:
        out_specs += [pl.BlockSpec((D_MODEL, tn), lambda i, j: (0, j))] * 5
        out_shape += [jax.ShapeDtypeStruct((D_MODEL, ATTN_WIDTH), BF16)] * 5
    outs = pl.pallas_call(
        functools.partial(_inproj_kernel, emit_w16=emit_w16),
        grid=(m // tm, nb),
        in_specs=[
            pl.BlockSpec((tm, D_MODEL), lambda i, j: (i, 0)),
            pl.BlockSpec((1, D_MODEL), lambda i, j: (0, 0)),
            *wspecs,
            pl.BlockSpec((1, LANES), lambda i, j: (0, 0)),
            pl.BlockSpec((1, LANES), lambda i, j: (0, 0)),
        ],
        out_specs=out_specs,
        out_shape=out_shape,
        scratch_shapes=[pltpu.VMEM((tm, D_MODEL), BF16)],
        compiler_params=_params("parallel", "arbitrary"),
        name="in_proj",
    )(x, g, *ws, qg, kg)
    return (outs[:4], outs[4:]) if emit_w16 else outs


def _attn_kernel(q_ref, k_ref, vt_ref, km_ref, vmt_ref, lq1_ref, lk1_ref, lq2_ref, lk2_ref, sg_ref,
                 *rest, tq, tk, cw, n_cast, tiles_per_step):
    w32_refs, o_ref, w16_refs = rest[:n_cast], rest[n_cast], rest[n_cast + 1:2 * n_cast + 1]
    m_sc, acc_sc, s0_sc, s1_sc = rest[2 * n_cast + 1:]
    n_chunks = 2 * tq // cw
    i = pl.program_id(2)

    def scores(blk, s_ref, qs):
        kb = k_ref[blk * tk:(blk + 1) * tk, :]
        for c in range(n_chunks):
            s_ref[c] = jnp.dot(kb, qs[:, c * cw:(c + 1) * cw], preferred_element_type=F32)

    def with_ones(vt):
        return jnp.concatenate([vt, jnp.ones((DENOM_ROWS, vt.shape[1]), BF16)], axis=0)

    def online_update(s, vt, m_old, acc_old):
        m_new = jnp.maximum(m_old, jnp.max(s, axis=0, keepdims=True))
        alpha = jnp.exp2(m_old - m_new)
        p = jnp.exp2(s - m_new)
        return m_new, alpha * acc_old + jnp.dot(with_ones(vt), p.astype(BF16), preferred_element_type=F32)

    def softmax_pv(blk, s_ref, diag):
        off = blk * tk
        for c in range(n_chunks):
            cs = slice(c * cw, (c + 1) * cw)
            nk, masked = tk, False
            if diag is not None:
                q0 = (c * cw) % tq
                nk = min(tk, q0 + cw - diag * tk)
                if nk <= 0:
                    continue
                masked = (diag * tk + nk - 1) // CHUNK > q0 // CHUNK
            s = s_ref[c, 0:nk, :]
            if masked:
                kchunk = (lax.broadcasted_iota(jnp.int32, (nk, cw), 0) + diag * tk) // CHUNK
                qchunk = (lax.broadcasted_iota(jnp.int32, (nk, cw), 1) + q0) // CHUNK
                s = jnp.where(kchunk <= qchunk, s, NEG_INF)
            m_sc[:, cs], acc_sc[:, cs] = online_update(s, vt_ref[:, off:off + nk], m_sc[:, cs], acc_sc[:, cs])

    def run(tile, r0):
        qt = q_ref[r0:r0 + tq, :].astype(F32).T
        row = lax.broadcasted_iota(jnp.int32, qt.shape, 0)
        qs = jnp.concatenate([jnp.where(row < QK_DIM, qt, 0.0),
                              jnp.where(row < QK_DIM, 0.0, qt)], axis=1).astype(BF16)
        per_tile = tq // tk
        blocks = [(b, None) for b in range(tile * per_tile)] + [(tile * per_tile + d, d) for d in range(per_tile)]
        bufs = (s0_sc, s1_sc)
        scores(blocks[0][0], bufs[0], qs)
        s = jnp.dot(km_ref[...], qs, preferred_element_type=F32)
        m_sc[...], acc_sc[...] = online_update(s, vmt_ref[...], jnp.full(m_sc.shape, NEG_INF, F32),
                                               jnp.zeros(acc_sc.shape, F32))
        for t, (blk, diag) in enumerate(blocks):
            if t + 1 < len(blocks):
                scores(blocks[t + 1][0], bufs[(t + 1) % 2], qs)
            softmax_pv(blk, bufs[t % 2], diag)

        lam = (jnp.exp(jnp.sum(lq1_ref[...] * lk1_ref[...], axis=-1, keepdims=True))
               - jnp.exp(jnp.sum(lq2_ref[...] * lk2_ref[...], axis=-1, keepdims=True)) + LAM_INIT)
        acc = acc_sc[...]
        o = acc[:HEAD_DIM] * (1.0 / acc[HEAD_DIM:HEAD_DIM + 1])
        od = o[:, :tq] - lam * o[:, tq:]
        ms = jnp.mean(od * od, axis=0, keepdims=True)
        on = od * lax.rsqrt(ms + EPS) * sg_ref[...] * (1.0 - LAM_INIT)
        o_ref[r0:r0 + tq, :] = on.T.astype(BF16)

    def step(first_tile):
        for w32, w16 in zip(w32_refs, w16_refs):
            w16[...] = w32[...].astype(BF16)
        for t in range(tiles_per_step):
            run(first_tile + t, t * tq)

    for n in range(SEQ // (tq * tiles_per_step)):
        pl.when(i == n)(functools.partial(step, n * tiles_per_step))


def _attention(q, k, vt, km, vmt, lq1, lk1, lq2, lk2, sg, cast_ws, *, batch, tq, tk, cw, tiles_per_step):
    assert tq % tk == 0 and tk % cw == 0 and cw % CHUNK == 0
    nq = SEQ // (tq * tiles_per_step)
    steps = batch * N_HEADS * nq
    lspec = pl.BlockSpec((1, QK_DIM), lambda b, h, i: (0, 0))

    def slab(w):
        rep = next(r for r in (1, 2, 4, 8) if w.shape[0] * r % (BF16_ROWS * steps) == 0)
        return pl.BlockSpec((w.shape[0] * rep // steps, w.shape[1]),
                            lambda b, h, i: (((b * N_HEADS + h) * nq + i) // rep, 0))

    outs = pl.pallas_call(
        functools.partial(_attn_kernel, tq=tq, tk=tk, cw=cw, n_cast=len(cast_ws), tiles_per_step=tiles_per_step),
        grid=(batch, N_HEADS, nq),
        in_specs=[
            pl.BlockSpec((tq * tiles_per_step, HEAD_DIM), lambda b, h, i: (b * nq + i, h)),
            pl.BlockSpec((SEQ, HEAD_DIM), lambda b, h, i: (b, h)),
            pl.BlockSpec((HEAD_DIM, SEQ), lambda b, h, i: (h, b)),
            pl.BlockSpec((N_META, HEAD_DIM), lambda b, h, i: (0, h)),
            pl.BlockSpec((HEAD_DIM, N_META), lambda b, h, i: (h, 0)),
            lspec, lspec, lspec, lspec,
            pl.BlockSpec((HEAD_DIM, 1), lambda b, h, i: (0, 0)),
        ] + [slab(w) for w in cast_ws],
        out_specs=[pl.BlockSpec((tq * tiles_per_step, HEAD_DIM), lambda b, h, i: (b * nq + i, h))]
        + [slab(w) for w in cast_ws],
        out_shape=[jax.ShapeDtypeStruct((batch * SEQ, ATTN_WIDTH), BF16)]
        + [jax.ShapeDtypeStruct(w.shape, BF16) for w in cast_ws],
        scratch_shapes=[
            pltpu.VMEM((1, 2 * tq), F32),
            pltpu.VMEM((HEAD_DIM + DENOM_ROWS, 2 * tq), F32),
            pltpu.VMEM((2 * tq // cw, tk, cw), F32),
            pltpu.VMEM((2 * tq // cw, tk, cw), F32),
        ],
        compiler_params=_params("parallel", "parallel", "arbitrary"),
        name="diff_attn",
    )(q, k, vt, km, vmt, lq1, lk1, lq2, lk2, sg, *cast_ws)
    return outs[0], outs[1:]


def _conv_kernel(u_ref, prev_ref, halo0_ref, w_ref, b_ref, y_ref, win_sc, shift_sc, *, tl, rc, rb):
    i = pl.program_id(1)

    @pl.when(i == 0)
    def _():
        win_sc[0:HALO, :] = halo0_ref[...]

    @pl.when(i > 0)
    def _():
        win_sc[0:HALO, :] = prev_ref[...]

    win_sc[HALO:HALO + tl, :] = u_ref[...]

    nsh = HALO + tl - SUBLANES
    for s in range(1, SUBLANES):
        shift_sc[s - 1] = win_sc[s:s + nsh, :]

    first = HALO - (CONV_KERNEL - 1)
    @pl.loop(0, CONV_WIDTH // LANES)
    def _(c):
        cs = pl.ds(pl.multiple_of(c * LANES, LANES), LANES)
        for r0 in range(0, tl // rc, rb):
            accs = [jnp.broadcast_to(b_ref[:, cs], (rc, LANES))] * rb
            for t in range(CONV_KERNEL):
                wt = w_ref[t:t + 1, cs]
                for k in range(rb):
                    lo = (r0 + k) * rc + first + t
                    s, al = lo % SUBLANES, lo - lo % SUBLANES
                    tap = win_sc[al:al + rc, cs] if s == 0 else shift_sc[s - 1, al:al + rc, cs]
                    accs[k] = accs[k] + wt * tap
            for k in range(rb):
                y_ref[(r0 + k) * rc:(r0 + k + 1) * rc, cs] = accs[k]


def _conv(u, halo0, w, b, *, batch, tl, rc, rb):
    nt = SEQ // tl
    hb = tl // HALO
    return pl.pallas_call(
        functools.partial(_conv_kernel, tl=tl, rc=rc, rb=rb),
        grid=(batch, nt),
        in_specs=[
            pl.BlockSpec((tl, CONV_WIDTH), lambda bb, i: (bb * nt + i, 0)),
            pl.BlockSpec((HALO, CONV_WIDTH), lambda bb, i: (jnp.maximum((bb * nt + i) * hb - 1, 0), 0)),
            pl.BlockSpec((HALO, CONV_WIDTH), lambda bb, i: (0, 0)),
            pl.BlockSpec((CONV_KERNEL, CONV_WIDTH), lambda bb, i: (0, 0)),
            pl.BlockSpec((1, CONV_WIDTH), lambda bb, i: (0, 0)),
        ],
        out_specs=pl.BlockSpec((tl, CONV_WIDTH), lambda bb, i: (bb * nt + i, 0)),
        out_shape=jax.ShapeDtypeStruct((batch * SEQ, CONV_WIDTH), F32),
        scratch_shapes=[
            pltpu.VMEM((HALO + tl, CONV_WIDTH), F32),
            pltpu.VMEM((SUBLANES - 1, HALO + tl - SUBLANES, CONV_WIDTH), F32),
        ],
        compiler_params=_params("parallel", "arbitrary"),
        name="conv",
    )(u, u, halo0, w, b)


def _outproj_kernel(x_ref, a_ref, y_ref, cng_ref, wa_ref, wc_ref, o_ref):
    att = jnp.dot(a_ref[...], wa_ref[...], preferred_element_type=F32)
    yn = _rms(y_ref[...], cng_ref[...])
    oc = (yn * _sigmoid(yn)).astype(BF16)
    o_ref[...] = x_ref[...] + att + jnp.dot(oc, wc_ref[...], preferred_element_type=F32)


def _out_proj(x, oa, y, cng, w_out, *, tm):
    m = x.shape[0]
    return pl.pallas_call(
        _outproj_kernel,
        grid=(m // tm,),
        in_specs=[
            pl.BlockSpec((tm, D_MODEL), lambda i: (i, 0)),
            pl.BlockSpec((tm, ATTN_WIDTH), lambda i: (i, 0)),
            pl.BlockSpec((tm, CONV_WIDTH), lambda i: (i, 0)),
            pl.BlockSpec((1, CONV_WIDTH), lambda i: (0, 0)),
            pl.BlockSpec((ATTN_WIDTH, D_MODEL), lambda i: (0, 0)),
            pl.BlockSpec((CONV_WIDTH, D_MODEL), lambda i: (1, 0)),
        ],
        out_specs=pl.BlockSpec((tm, D_MODEL), lambda i: (i, 0)),
        out_shape=jax.ShapeDtypeStruct((m, D_MODEL), F32),
        compiler_params=_params("parallel"),
        name="out_proj",
    )(x, oa, y, cng, w_out, w_out)


def kernel(x, meta_tokens, ffn1_norm_g, ffn1_w_gate, ffn1_w_up, ffn1_w_down, mix_norm_g, w_in, q_norm_g,
           k_norm_g, lambda_q1, lambda_k1, lambda_q2, lambda_k2, attn_subln_g, conv_w, conv_b, conv_norm_g,
           w_out, ffn2_norm_g, ffn2_w_gate, ffn2_w_up, ffn2_w_down, final_norm_g):
    batch = x.shape[0]
    xr = x.reshape(batch * SEQ, D_MODEL)

    w1g, w1u, w1d = ffn1_w_gate[0], ffn1_w_up[0], ffn1_w_down[0]
    w2g, w2u, w2d = ffn2_w_gate[0], ffn2_w_up[0], ffn2_w_down[0]
    win, wout = w_in[0], w_out[0]
    qg = jnp.tile(q_norm_g[0] * (QK_DIM ** -0.5 * LOG2E), 2)[None, :]
    kg = jnp.tile(k_norm_g[0], 2)[None, :]
    sg = attn_subln_g[0][:, None]
    cw = conv_w[0].reshape(CONV_KERNEL, CONV_WIDTH)

    m1, (w1g, w1u, w1d) = _ffn(meta_tokens, ffn1_norm_g, w1g, w1u, w1d, final_norm_g,
                               tm=N_META, tf=META_FFN_TF, final_norm=False, emit_w16=True)
    (_, km, vmt, um), win16 = _in_proj(m1, mix_norm_g, win, qg, kg, tm=N_META, tn=PROJ_TN, emit_w16=True)
    halo0 = jnp.concatenate([jnp.zeros((HALO - N_META, CONV_WIDTH), F32), um], axis=0)

    x1 = _ffn(xr, ffn1_norm_g, w1g, w1u, w1d, final_norm_g, tm=FFN_TM, tf=FFN_TF, final_norm=False)
    q, k, vt, u = _in_proj(x1, mix_norm_g, win16, qg, kg, tm=PROJ_TM, tn=PROJ_TN)
    oa, (w2g, w2u, w2d, wout) = _attention(q, k, vt, km, vmt, lambda_q1, lambda_k1, lambda_q2, lambda_k2, sg,
                                           [w2g, w2u, w2d, wout], batch=batch,
                                           tq=ATTN_TQ, tk=ATTN_TK, cw=ATTN_CW,
                                           tiles_per_step=ATTN_TILES_PER_STEP)
    yc = _conv(u, halo0, cw, conv_b, batch=batch, tl=CONV_TL, rc=CONV_RC, rb=CONV_RB)
    x2 = _out_proj(x1, oa, yc, conv_norm_g, wout, tm=OUT_TM)
    y = _ffn(x2, ffn2_norm_g, w2g, w2u, w2d, final_norm_g, tm=FFN_TM, tf=FFN_TF, final_norm=True)
    return y.reshape(batch, SEQ, D_MODEL)
```
